```python
import jax, jax.numpy as jnp
from jax import lax
import numpy as np

D_MODEL = 2048
BATCH = 8
SEQ = 2048
DEPTH = 2

N_MIXERS = 2
N_HEADS = 16
HEAD_DIM = D_MODEL // N_HEADS
Q_BLOCK = 128
MOBA_BLOCK = 256
MOBA_TOPK = 3
MOBA_Q_CHUNK = 8
D_FF = 5632
N_EXPERTS = 8
MOE_TOPK = 2
D_FF_EXPERT = 7168
EPS = 1e-6

kernel_name = "hybrid_stickbreak_moba_moe_adaln"


def rms_norm(x, g):
    xf = x.astype(jnp.float32)
    y = xf * lax.rsqrt(jnp.mean(xf * xf, axis=-1, keepdims=True) + EPS)
    return (y * g.astype(jnp.float32)).astype(x.dtype)


def alibi_slopes(n_heads):
    return jnp.exp2(-8.0 * jnp.arange(1, n_heads + 1, dtype=jnp.float32) / n_heads)


def swiglu(h, w_in, w_out):
    gu = h @ w_in
    g, u = jnp.split(gu, 2, axis=-1)
    return (jax.nn.silu(g) * u) @ w_out


def stick_breaking_attention(q, k, v):
    B, H, S, d = q.shape
    scale = d ** -0.5
    outs = []
    for start in range(0, S, Q_BLOCK):
        end = start + Q_BLOCK
        qb = q[:, :, start:end]
        kb = k[:, :, :end]
        vb = v[:, :, :end]
        z = jnp.einsum('bhqd,bhkd->bhqk', qb, kb, preferred_element_type=jnp.float32) * scale
        t_pos = start + jnp.arange(Q_BLOCK)[:, None]
        s_pos = jnp.arange(end)[None, :]
        past = s_pos < t_pos
        log_keep = jnp.where(past, jax.nn.log_sigmoid(-z), 0.0)
        later = lax.cumsum(log_keep, axis=3, reverse=True) - log_keep
        a = jnp.where(past, jnp.exp(jax.nn.log_sigmoid(z) + later), 0.0)
        outs.append(jnp.einsum('bhqk,bhkd->bhqd', a.astype(vb.dtype), vb))
    return jnp.concatenate(outs, axis=2)


def moba_attention(q, k, v, slopes):
    B, H, S, d = q.shape
    scale = d ** -0.5
    nb = -(-S // MOBA_BLOCK)
    pad = nb * MOBA_BLOCK - S
    kp = jnp.pad(k, ((0, 0), (0, 0), (0, pad), (0, 0)))
    vp = jnp.pad(v, ((0, 0), (0, 0), (0, pad), (0, 0)))
    kblk = kp.reshape(B, H, nb, MOBA_BLOCK, d)
    vblk = vp.reshape(B, H, nb, MOBA_BLOCK, d)
    kmean = jnp.mean(kblk.astype(jnp.float32), axis=3)
    t = jnp.arange(S)
    own = t // MOBA_BLOCK
    gate = jnp.einsum('bhsd,bhnd->bhsn', q.astype(jnp.float32), kmean)
    fully_past = jnp.arange(nb)[None, :] < own[:, None]
    gate = jnp.where(fully_past, gate, -jnp.inf)
    n_sel = min(MOBA_TOPK, nb)
    top_val, top_idx = lax.top_k(gate, n_sel)
    sel_valid = jnp.isfinite(top_val)
    own_idx = jnp.broadcast_to(own[None, None, :, None], (B, H, S, 1)).astype(top_idx.dtype)
    idx = jnp.concatenate([top_idx, own_idx], axis=-1)
    valid = jnp.concatenate([sel_valid, jnp.ones((B, H, S, 1), dtype=bool)], axis=-1)

    nc = S // MOBA_Q_CHUNK

    def to_chunks(a):
        a = a.reshape(B, H, nc, MOBA_Q_CHUNK, *a.shape[3:])
        return jnp.moveaxis(a, 2, 0)

    b_ix = jnp.arange(B)[:, None, None, None]
    h_ix = jnp.arange(H)[None, :, None, None]
    offs = jnp.arange(MOBA_BLOCK)
    slope = slopes[None, :, None, None, None]

    def chunk(args):
        qc, idxc, validc, tc = args
        kg = kblk[b_ix, h_ix, idxc]
        vg = vblk[b_ix, h_ix, idxc]
        pos = idxc[..., None] * MOBA_BLOCK + offs
        dist = tc[None, None, :, None, None] - pos
        mask = validc[..., None] & (dist >= 0)
        s = jnp.einsum('bhqd,bhqnkd->bhqnk', qc, kg, preferred_element_type=jnp.float32) * scale
        s = s - slope * dist.astype(jnp.float32)
        s = jnp.where(mask, s, -jnp.inf)
        p = jax.nn.softmax(s.reshape(B, H, MOBA_Q_CHUNK, -1), axis=-1).reshape(s.shape)
        return jnp.einsum('bhqnk,bhqnkd->bhqd', p.astype(vg.dtype), vg)

    out = lax.map(chunk, (to_chunks(q), to_chunks(idx), to_chunks(valid),
                          t.reshape(nc, MOBA_Q_CHUNK)))
    return jnp.moveaxis(out, 0, 2).reshape(B, H, S, d)


def moe_swiglu(h, w_router, w_in, w_out):
    logits = jnp.einsum('nd,de->ne', h, w_router, preferred_element_type=jnp.float32)
    top_val, top_idx = lax.top_k(logits, MOE_TOPK)
    top_w = jax.nn.softmax(top_val, axis=-1)
    gates = jnp.sum(jax.nn.one_hot(top_idx, N_EXPERTS, dtype=jnp.float32) * top_w[..., None], axis=1)
    y = jnp.zeros_like(h)
    for e in range(N_EXPERTS):
        y = y + gates[:, e:e + 1].astype(h.dtype) * swiglu(h, w_in[e], w_out[e])
    return y


def setup_inputs(seed: int = 0) -> dict:
    key = jax.random.key(seed)
    ks = jax.random.split(key, 16)
    n_even = (DEPTH + 1) // 2
    n_odd = DEPTH // 2
    D = D_MODEL
    nrm = jax.random.normal
    f32 = jnp.float32
    return {
        "x": nrm(ks[0], (BATCH, SEQ, D), f32),
        "c": nrm(ks[1], (BATCH, D), f32),
        "w_ada": nrm(ks[2], (DEPTH, D, 6 * D), f32) * (0.5 * D ** -0.5),
        "b_ada": nrm(ks[3], (DEPTH, 6 * D), f32) * 0.02,
        "attn_norm": 1.0 + 0.1 * nrm(ks[4], (DEPTH, D), f32),
        "ffn_norm": 1.0 + 0.1 * nrm(ks[5], (DEPTH, D), f32),
        "w_qkv": nrm(ks[6], (DEPTH, D, 3 * D), f32) * D ** -0.5,
        "w_o": nrm(ks[7], (DEPTH, D, D), f32) * D ** -0.5,
        "qk_norm_q": 1.0 + 0.1 * nrm(ks[8], (n_odd, HEAD_DIM), f32),
        "qk_norm_k": 1.0 + 0.1 * nrm(ks[9], (n_odd, HEAD_DIM), f32),
        "ffn_w_in": nrm(ks[10], (n_even, D, 2 * D_FF), f32) * D ** -0.5,
        "ffn_w_out": nrm(ks[11], (n_even, D_FF, D), f32) * D_FF ** -0.5,
        "router_w": nrm(ks[12], (n_odd, D, N_EXPERTS), f32) * D ** -0.5,
        "moe_w_in": nrm(ks[13], (n_odd, N_EXPERTS, D, 2 * D_FF_EXPERT), f32) * D ** -0.5,
        "moe_w_out": nrm(ks[14], (n_odd, N_EXPERTS, D_FF_EXPERT, D), f32) * D_FF_EXPERT ** -0.5,
    }


def reference(x, c, w_ada, b_ada, attn_norm, ffn_norm, w_qkv, w_o, qk_norm_q, qk_norm_k,
              ffn_w_in, ffn_w_out, router_w, moe_w_in, moe_w_out):
    B, S, D = x.shape
    slopes = alibi_slopes(N_HEADS)
    for i in range(DEPTH):
        mod = (c @ w_ada[i] + b_ada[i])[:, None, :]
        sh_a, sc_a, g_a, sh_f, sc_f, g_f = jnp.split(mod, 6, axis=-1)
        h = rms_norm(x, attn_norm[i]) * (1.0 + sc_a) + sh_a
        qkv = (h @ w_qkv[i]).reshape(B, S, 3, N_HEADS, HEAD_DIM)
        q = jnp.swapaxes(qkv[:, :, 0], 1, 2)
        k = jnp.swapaxes(qkv[:, :, 1], 1, 2)
        v = jnp.swapaxes(qkv[:, :, 2], 1, 2)
        if i % N_MIXERS == 0:
            o = stick_breaking_attention(q, k, v)
        else:
            j = i // N_MIXERS
            q = rms_norm(q, qk_norm_q[j])
            k = rms_norm(k, qk_norm_k[j])
            o = moba_attention(q, k, v, slopes)
        o = jnp.swapaxes(o, 1, 2).reshape(B, S, D)
        x = x + g_a * (o @ w_o[i])
        h = rms_norm(x, ffn_norm[i]) * (1.0 + sc_f) + sh_f
        if i % 2 == 0:
            y = swiglu(h, ffn_w_in[i // 2], ffn_w_out[i // 2])
        else:
            y = moe_swiglu(h.reshape(B * S, D), router_w[i // 2], moe_w_in[i // 2],
                           moe_w_out[i // 2]).reshape(B, S, D)
        x = x + g_f * y
    return x
```

```python
import functools

import jax
import jax.numpy as jnp
from jax import lax
from jax.experimental import pallas as pl
from jax.experimental.pallas import tpu as pltpu

F32 = jnp.float32
BF16 = jnp.bfloat16

HEAD_DIM = 128
MOBA_BLOCK = 256
MOBA_TOPK = 3
MOE_TOPK = 2
EPS = 1e-6
MASKED = -1e30
LANES = 128
VMEM_LIMIT = 56 * 1024 * 1024


def _tile(dim, pref):
    t = min(dim, pref)
    while dim % t:
        t //= 2
    return t


def _params(*sem):
    return pltpu.CompilerParams(dimension_semantics=sem, vmem_limit_bytes=VMEM_LIMIT)


def _dot(a, b):
    return jnp.dot(a, b, preferred_element_type=F32)


def _dot_nt(a, b):
    return lax.dot_general(a, b, (((1,), (1,)), ((), ())), preferred_element_type=F32)


def _split_bf16(x):
    hi = x.astype(BF16)
    lo = (x - hi.astype(F32)).astype(BF16)
    return hi, lo


def _adaln_kernel(c_ref, w_ref, b_ref, o_ref):
    acc = _dot(c_ref[...].astype(BF16), w_ref[0].astype(BF16))
    o_ref[0] = acc + b_ref[0]


def _adaln(c, w_ada, b_ada):
    L, D, D6 = w_ada.shape
    B = c.shape[0]
    tn = _tile(D6, 1024)
    return pl.pallas_call(
        _adaln_kernel,
        grid=(L, D6 // tn),
        in_specs=[
            pl.BlockSpec((B, D), lambda l, j: (0, 0)),
            pl.BlockSpec((1, D, tn), lambda l, j: (l, 0, j)),
            pl.BlockSpec((1, 1, tn), lambda l, j: (l, 0, j)),
        ],
        out_specs=pl.BlockSpec((1, B, tn), lambda l, j: (l, 0, j)),
        out_shape=jax.ShapeDtypeStruct((L, B, D6), F32),
        compiler_params=_params("parallel", "parallel"),
        name="adaln",
    )(c, w_ada, b_ada.reshape(L, 1, D6))


def _norm_mod(x, gain, scale, shift):
    ms = jnp.mean(x * x, axis=-1, keepdims=True)
    y = x * lax.rsqrt(ms + EPS) * gain
    return y * (1.0 + scale) + shift


def _qkv_kernel(x_ref, g_ref, sc_ref, sh_ref, w_ref, *rest, qk_norm, d_model):
    if qk_norm:
        gqk_ref, o_ref, h_scr = rest
    else:
        o_ref, h_scr = rest
    j = pl.program_id(1)
    tn = o_ref.shape[1]

    @pl.when(j == 0)
    def _():
        h = _norm_mod(x_ref[...], g_ref[...], sc_ref[0, 0], sh_ref[0, 0])
        h_scr[...] = h.astype(BF16)

    acc = _dot(h_scr[...], w_ref[...])
    if not qk_norm:
        o_ref[...] = acc.astype(o_ref.dtype)
    else:
        is_qk = j * tn < 2 * d_model

        @pl.when(is_qk)
        def _():
            for hh in range(tn // HEAD_DIM):
                sl = slice(hh * HEAD_DIM, (hh + 1) * HEAD_DIM)
                blk = acc[:, sl]
                ms = jnp.mean(blk * blk, axis=-1, keepdims=True)
                o_ref[:, sl] = (blk * lax.rsqrt(ms + EPS) * gqk_ref[:, sl]).astype(o_ref.dtype)

        @pl.when(jnp.logical_not(is_qk))
        def _():
            o_ref[...] = acc.astype(o_ref.dtype)


def _qkv_proj(x2d, gain, mod_l, w_bf16, seq, gqk=None):
    N, D = x2d.shape
    D3 = w_bf16.shape[1]
    tm = _tile(seq, 512)
    tn = _tile(D, 1024)
    assert (2 * D) % tn == 0 and tn % HEAD_DIM == 0
    rows_per_batch = seq // tm
    qk_norm = gqk is not None
    in_specs = [
        pl.BlockSpec((tm, D), lambda i, j: (i, 0)),
        pl.BlockSpec((1, D), lambda i, j: (0, 0)),
        pl.BlockSpec((1, 1, 1, D), lambda i, j: (i // rows_per_batch, 1, 0, 0)),
        pl.BlockSpec((1, 1, 1, D), lambda i, j: (i // rows_per_batch, 0, 0, 0)),
        pl.BlockSpec((D, tn), lambda i, j: (0, j)),
    ]
    args = [x2d, gain.reshape(1, D), mod_l, mod_l, w_bf16]
    if qk_norm:
        in_specs.append(pl.BlockSpec((1, tn), lambda i, j: (0, j)))
        args.append(gqk)
    return pl.pallas_call(
        functools.partial(_qkv_kernel, qk_norm=qk_norm, d_model=D),
        grid=(N // tm, D3 // tn),
        in_specs=in_specs,
        out_specs=pl.BlockSpec((tm, tn), lambda i, j: (i, j)),
        out_shape=jax.ShapeDtypeStruct((N, D3), BF16),
        scratch_shapes=[pltpu.VMEM((tm, D), BF16)],
        compiler_params=_params("parallel", "arbitrary"),
        name="qkv_norm" if qk_norm else "qkv",
    )(*args)


def _sb_kernel(q_ref, k_ref, v_ref, o_ref, *, scale):
    qi = pl.program_id(2)
    tq = q_ref.shape[1]
    q = q_ref[0]
    row = lax.broadcasted_iota(jnp.int32, (tq, tq), 0)
    col = lax.broadcasted_iota(jnp.int32, (tq, tq), 1)
    past = col < row
    later_keys = jnp.where(row > col, 1.0, 0.0).astype(BF16)

    def block(start, carry, acc, diag):
        kb = k_ref[0, pl.ds(start, tq), :]
        vb = v_ref[0, pl.ds(start, tq), :]
        z = _dot_nt(q, kb) * scale
        log_beta = jnp.minimum(z, 0.0) - jnp.log(1.0 + jnp.exp(-jnp.abs(z)))
        log_keep = log_beta - z
        if diag:
            log_keep = jnp.where(past, log_keep, 0.0)
        hi, lo = _split_bf16(log_keep)
        later = _dot(hi, later_keys) + _dot(lo, later_keys)
        a = jnp.exp(log_beta + later + carry)
        if diag:
            a = jnp.where(past, a, 0.0)
        acc = acc + _dot(a.astype(BF16), vb)
        carry = carry + later[:, 0:1] + log_keep[:, 0:1]
        return carry, acc

    carry0 = jnp.zeros((tq, 1), F32)
    acc0 = jnp.zeros((tq, HEAD_DIM), F32)
    carry, acc = block(pl.multiple_of(qi * tq, tq), carry0, acc0, True)

    def body(it, ca):
        start = pl.multiple_of((qi - 1 - it) * tq, tq)
        return block(start, ca[0], ca[1], False)

    carry, acc = lax.fori_loop(0, qi, body, (carry, acc))
    o_ref[0] = acc.astype(o_ref.dtype)


def _stickbreak_attention(qkv, n_heads):
    B, S, _ = qkv.shape
    H = n_heads
    tq = _tile(S, 256)
    return pl.pallas_call(
        functools.partial(_sb_kernel, scale=HEAD_DIM ** -0.5),
        grid=(B, H, S // tq),
        in_specs=[
            pl.BlockSpec((1, tq, HEAD_DIM), lambda b, h, i: (b, i, h)),
            pl.BlockSpec((1, S, HEAD_DIM), lambda b, h, i: (b, 0, H + h)),
            pl.BlockSpec((1, S, HEAD_DIM), lambda b, h, i: (b, 0, 2 * H + h)),
        ],
        out_specs=pl.BlockSpec((1, tq, HEAD_DIM), lambda b, h, i: (b, i, h)),
        out_shape=jax.ShapeDtypeStruct((B, S, H * HEAD_DIM), BF16),
        compiler_params=_params("parallel", "parallel", "arbitrary"),
        name="stickbreak_attn",
    )(qkv, qkv, qkv)


def _moba_kernel(q_ref, k_ref, v_ref, o_ref, kmean_scr, *, scale, n_heads, n_blocks):
    h = pl.program_id(1)
    qi = pl.program_id(2)
    tq = q_ref.shape[1]

    @pl.when(qi == 0)
    def _():
        kmean_scr[...] = jnp.zeros_like(kmean_scr)
        for n in range(n_blocks):
            kb = k_ref[0, n * MOBA_BLOCK:(n + 1) * MOBA_BLOCK, :].astype(F32)
            kmean_scr[n:n + 1, :] = jnp.mean(kb, axis=0, keepdims=True)

    q = q_ref[0]
    slope = jnp.exp2(jnp.full((1, 1), h + 1, jnp.int32).astype(F32) * (-8.0 / n_heads))

    km_hi, km_lo = _split_bf16(kmean_scr[...])
    gate = _dot_nt(q, km_hi) + _dot_nt(q, km_lo)
    lane = lax.broadcasted_iota(jnp.int32, (tq, LANES), 1)
    fully_past = lane < qi
    beaten = jnp.zeros((tq, LANES), F32)
    for m in range(n_blocks - 1):
        gm = gate[:, m:m + 1]
        ahead = (gm > gate) | ((gm == gate) & (m < lane))
        beaten = beaten + jnp.where(ahead & (m < qi), 1.0, 0.0)
    chosen = jnp.where(fully_past & (beaten < MOBA_TOPK), 1.0, 0.0)

    rel = (lax.broadcasted_iota(jnp.int32, (tq, tq), 0)
           - lax.broadcasted_iota(jnp.int32, (tq, tq), 1)).astype(F32)

    def block(j, keep, m_run, l_run, acc):
        start = pl.multiple_of(j * MOBA_BLOCK, MOBA_BLOCK)
        kb = k_ref[0, pl.ds(start, MOBA_BLOCK), :]
        vb = v_ref[0, pl.ds(start, MOBA_BLOCK), :]
        dist = rel + jnp.full((1, 1), (qi - j) * MOBA_BLOCK, jnp.int32).astype(F32)
        s = _dot_nt(q, kb) * scale - slope * dist
        s = jnp.where(keep(dist), s, MASKED)
        m_new = jnp.maximum(m_run, jnp.max(s, axis=-1, keepdims=True))
        alpha = jnp.exp(m_run - m_new)
        p = jnp.exp(s - m_new)
        l_new = alpha * l_run + jnp.sum(p, axis=-1, keepdims=True)
        acc = alpha * acc + _dot(p.astype(BF16), vb)
        return m_new, l_new, acc

    m0 = jnp.full((tq, 1), MASKED, F32)
    l0 = jnp.zeros((tq, 1), F32)
    acc0 = jnp.zeros((tq, HEAD_DIM), F32)
    state = block(qi, lambda dist: dist >= 0.0, m0, l0, acc0)

    def body(j, st):
        picked = jnp.max(jnp.where(lane == j, chosen, 0.0), axis=-1, keepdims=True)
        return block(j, lambda dist: picked > 0.0, *st)

    m_run, l_run, acc = lax.fori_loop(0, qi, body, state)
    o_ref[0] = (acc / l_run).astype(o_ref.dtype)


def _moba_attention(qkv, n_heads):
    B, S, _ = qkv.shape
    H = n_heads
    assert S % MOBA_BLOCK == 0
    nb = S // MOBA_BLOCK
    assert nb <= LANES
    return pl.pallas_call(
        functools.partial(_moba_kernel, scale=HEAD_DIM ** -0.5, n_heads=H, n_blocks=nb),
        grid=(B, H, nb),
        in_specs=[
            pl.BlockSpec((1, MOBA_BLOCK, HEAD_DIM), lambda b, h, i: (b, i, h)),
            pl.BlockSpec((1, S, HEAD_DIM), lambda b, h, i: (b, 0, H + h)),
            pl.BlockSpec((1, S, HEAD_DIM), lambda b, h, i: (b, 0, 2 * H + h)),
        ],
        out_specs=pl.BlockSpec((1, MOBA_BLOCK, HEAD_DIM), lambda b, h, i: (b, i, h)),
        out_shape=jax.ShapeDtypeStruct((B, S, H * HEAD_DIM), BF16),
        scratch_shapes=[pltpu.VMEM((LANES, HEAD_DIM), F32)],
        compiler_params=_params("parallel", "parallel", "arbitrary"),
        name="moba_attn",
    )(qkv, qkv, qkv)


def _outproj_kernel(o_ref, w_ref, x_ref, ga_ref, gn_ref, sc_ref, sh_ref, *rest, n_experts):
    if n_experts:
        wr_hi_ref, wr_lo_ref, x1_ref, h_ref, route_ref = rest
    else:
        x1_ref, h_ref = rest
    x1 = x_ref[...] + ga_ref[0, 0] * _dot(o_ref[...], w_ref[...])
    x1_ref[...] = x1
    h = _norm_mod(x1, gn_ref[...], sc_ref[0, 0], sh_ref[0, 0])
    h_ref[...] = h.astype(h_ref.dtype)
    if n_experts:
        h_hi, h_lo = _split_bf16(h)
        logits = (_dot(h_hi, wr_hi_ref[...]) + _dot(h_hi, wr_lo_ref[...])
                  + _dot(h_lo, wr_hi_ref[...]))
        lane = lax.broadcasted_iota(jnp.int32, logits.shape, 1).astype(F32)
        lg = jnp.where(lane < n_experts, logits, -jnp.inf)
        v1 = jnp.max(lg, axis=-1, keepdims=True)
        i1 = jnp.min(jnp.where(lg == v1, lane, float(LANES)), axis=-1, keepdims=True)
        lg2 = jnp.where(lane == i1, -jnp.inf, lg)
        v2 = jnp.max(lg2, axis=-1, keepdims=True)
        i2 = jnp.min(jnp.where(lg2 == v2, lane, float(LANES)), axis=-1, keepdims=True)
        e2 = jnp.exp(v2 - v1)
        w1 = 1.0 / (1.0 + e2)
        w2 = e2 / (1.0 + e2)
        route_ref[...] = jnp.where(
            lane == 0.0, i1,
            jnp.where(lane == 1.0, i2, jnp.where(lane == 2.0, w1, jnp.where(lane == 3.0, w2, 0.0))))


def _outproj(o2d, w_bf16, x2d, gain, mod_l, seq, h_dtype, router_w=None):
    N, D = x2d.shape
    tm = _tile(seq, 256)
    rows_per_batch = seq // tm
    mod_spec = lambda k: pl.BlockSpec((1, 1, 1, D), lambda i: (i // rows_per_batch, k, 0, 0))
    row_spec = pl.BlockSpec((tm, D), lambda i: (i, 0))
    in_specs = [
        row_spec,
        pl.BlockSpec((D, D), lambda i: (0, 0)),
        row_spec,
        mod_spec(2), pl.BlockSpec((1, D), lambda i: (0, 0)), mod_spec(4), mod_spec(3),
    ]
    args = [o2d, w_bf16, x2d, mod_l, gain.reshape(1, D), mod_l, mod_l]
    out_specs = [row_spec, row_spec]
    out_shape = [jax.ShapeDtypeStruct((N, D), F32), jax.ShapeDtypeStruct((N, D), h_dtype)]
    n_experts = 0
    if router_w is not None:
        n_experts = router_w.shape[1]
        assert n_experts <= LANES
        wr = jnp.zeros((D, LANES), F32).at[:, :n_experts].set(router_w)
        wr_hi = wr.astype(BF16)
        wr_lo = (wr - wr_hi.astype(F32)).astype(BF16)
        in_specs += [pl.BlockSpec((D, LANES), lambda i: (0, 0))] * 2
        args += [wr_hi, wr_lo]
        out_specs.append(pl.BlockSpec((tm, LANES), lambda i: (i, 0)))
        out_shape.append(jax.ShapeDtypeStruct((N, LANES), F32))
    return pl.pallas_call(
        functools.partial(_outproj_kernel, n_experts=n_experts),
        grid=(N // tm,),
        in_specs=in_specs,
        out_specs=out_specs,
        out_shape=out_shape,
        compiler_params=_params("parallel"),
        name="outproj_route" if n_experts else "outproj",
    )(*args)


def _swiglu_act(h, wg, wu):
    g = _dot(h, wg)
    u = _dot(h, wu)
    return (g * (1.0 / (1.0 + jnp.exp(-g))) * u).astype(BF16)


def _ffn_kernel(h_ref, wg_ref, wu_ref, wo_ref, x_ref, gf_ref, out_ref, acc_scr):
    j = pl.program_id(1)

    @pl.when(j == 0)
    def _():
        acc_scr[...] = jnp.zeros_like(acc_scr)

    acc_scr[...] += _dot(_swiglu_act(h_ref[...], wg_ref[...], wu_ref[...]), wo_ref[...])

    @pl.when(j == pl.num_programs(1) - 1)
    def _():
        out_ref[...] = x_ref[...] + gf_ref[0, 0] * acc_scr[...]


def _dense_ffn(h_bf16, w_in_bf16, w_out_bf16, x2d, mod_l, seq):
    N, D = x2d.shape
    d_ff = w_out_bf16.shape[0]
    tm = _tile(seq, 512)
    tn = _tile(d_ff, 512)
    nj = d_ff // tn
    rows_per_batch = seq // tm
    row_spec = pl.BlockSpec((tm, D), lambda i, j: (i, 0))
    return pl.pallas_call(
        _ffn_kernel,
        grid=(N // tm, nj),
        in_specs=[
            row_spec,
            pl.BlockSpec((D, tn), lambda i, j: (0, j)),
            pl.BlockSpec((D, tn), lambda i, j: (0, j + nj)),
            pl.BlockSpec((tn, D), lambda i, j: (j, 0)),
            row_spec,
            pl.BlockSpec((1, 1, 1, D), lambda i, j: (i // rows_per_batch, 5, 0, 0)),
        ],
        out_specs=row_spec,
        out_shape=jax.ShapeDtypeStruct((N, D), F32),
        scratch_shapes=[pltpu.VMEM((tm, D), F32)],
        compiler_params=_params("parallel", "arbitrary"),
        name="dense_ffn",
    )(h_bf16, w_in_bf16, w_in_bf16, w_out_bf16, x2d, mod_l)


def _moe_kernel(te_ref, tv_ref, rt_ref, h_hbm, wg_ref, wu_ref, wo_ref, rw_ref, out_ref,
                rows_f32, rows_bf16, acc_scr, sem):
    t = pl.program_id(0)
    j = pl.program_id(1)
    tm = rows_f32.shape[0]
    valid = tv_ref[t] == 1
    last = j == pl.num_programs(1) - 1

    def row_copy(r):
        tok = rt_ref[t * tm + r]
        return pltpu.make_async_copy(h_hbm.at[pl.ds(tok, 1), :], rows_f32.at[pl.ds(r, 1), :], sem)

    @pl.when(valid & (j == 0))
    def _():
        def start(r, c):
            row_copy(r).start()
            return c

        lax.fori_loop(0, tm, start, 0)

        def wait(r, c):
            row_copy(r).wait()
            return c

        lax.fori_loop(0, tm, wait, 0)
        rows_bf16[...] = rows_f32[...].astype(BF16)
        acc_scr[...] = jnp.zeros_like(acc_scr)

    @pl.when(valid)
    def _():
        acc_scr[...] += _dot(_swiglu_act(rows_bf16[...], wg_ref[0], wu_ref[0]), wo_ref[0])

    @pl.when(valid & last)
    def _():
        out_ref[...] = acc_scr[...] * rw_ref[...]

    @pl.when(jnp.logical_not(valid) & last)
    def _():
        out_ref[...] = jnp.zeros_like(out_ref)


def _moe_grouped(h_f32, w_in_bf16, w_out_bf16, tile_expert, tile_valid, row_token, row_weight, tm):
    N, D = h_f32.shape
    E, d_ff, _ = w_out_bf16.shape
    P = row_token.shape[0]
    tn = _tile(d_ff, 512)
    nj = d_ff // tn

    def ff_index(t, j, tv):
        return jnp.where(tv[t] == 1, j, nj - 1)

    return pl.pallas_call(
        _moe_kernel,
        grid_spec=pltpu.PrefetchScalarGridSpec(
            num_scalar_prefetch=3,
            grid=(P // tm, nj),
            in_specs=[
                pl.BlockSpec(memory_space=pl.ANY),
                pl.BlockSpec((1, D, tn), lambda t, j, te, tv, rt: (te[t], 0, ff_index(t, j, tv))),
                pl.BlockSpec((1, D, tn), lambda t, j, te, tv, rt: (te[t], 0, ff_index(t, j, tv) + nj)),
                pl.BlockSpec((1, tn, D), lambda t, j, te, tv, rt: (te[t], ff_index(t, j, tv), 0)),
                pl.BlockSpec((tm, 1), lambda t, j, te, tv, rt: (t, 0)),
            ],
            out_specs=pl.BlockSpec((tm, D), lambda t, j, te, tv, rt: (t, 0)),
            scratch_shapes=[
                pltpu.VMEM((tm, D), F32),
                pltpu.VMEM((tm, D), BF16),
                pltpu.VMEM((tm, D), F32),
                pltpu.SemaphoreType.DMA(()),
            ],
        ),
        out_shape=jax.ShapeDtypeStruct((P, D), F32),
        compiler_params=_params("arbitrary", "arbitrary"),
        name="moe_grouped",
    )(tile_expert, tile_valid, row_token, h_f32, w_in_bf16, w_in_bf16, w_out_bf16,
      row_weight.reshape(P, 1))


def _combine_kernel(p0_ref, p1_ref, ys_hbm, x_ref, gf_ref, out_ref, buf0, buf1, sem0, sem1):
    i = pl.program_id(0)
    tm = buf0.shape[0]

    def copies(r):
        c0 = pltpu.make_async_copy(ys_hbm.at[pl.ds(p0_ref[i * tm + r], 1), :],
                                   buf0.at[pl.ds(r, 1), :], sem0)
        c1 = pltpu.make_async_copy(ys_hbm.at[pl.ds(p1_ref[i * tm + r], 1), :],
                                   buf1.at[pl.ds(r, 1), :], sem1)
        return c0, c1

    def start(r, c):
        c0, c1 = copies(r)
        c0.start()
        c1.start()
        return c

    lax.fori_loop(0, tm, start, 0)

    def wait(r, c):
        c0, c1 = copies(r)
        c0.wait()
        c1.wait()
        return c

    lax.fori_loop(0, tm, wait, 0)
    out_ref[...] = x_ref[...] + gf_ref[0, 0] * (buf0[...] + buf1[...])


def _moe_combine(y_sorted, pos0, pos1, x2d, mod_l, seq):
    N, D = x2d.shape
    tm = _tile(seq, 256)
    rows_per_batch = seq // tm
    return pl.pallas_call(
        _combine_kernel,
        grid_spec=pltpu.PrefetchScalarGridSpec(
            num_scalar_prefetch=2,
            grid=(N // tm,),
            in_specs=[
                pl.BlockSpec(memory_space=pl.ANY),
                pl.BlockSpec((tm, D), lambda i, p0, p1: (i, 0)),
                pl.BlockSpec((1, 1, 1, D), lambda i, p0, p1: (i // rows_per_batch, 5, 0, 0)),
            ],
            out_specs=pl.BlockSpec((tm, D), lambda i, p0, p1: (i, 0)),
            scratch_shapes=[
                pltpu.VMEM((tm, D), F32),
                pltpu.VMEM((tm, D), F32),
                pltpu.SemaphoreType.DMA(()),
                pltpu.SemaphoreType.DMA(()),
            ],
        ),
        out_shape=jax.ShapeDtypeStruct((N, D), F32),
        compiler_params=_params("arbitrary"),
        name="moe_combine",
    )(pos0, pos1, y_sorted, x2d, mod_l)


def _routing_tables(route, n_experts, tm):
    N = route.shape[0]
    A = N * MOE_TOPK
    n_tiles = A // tm + n_experts
    expert = route[:, :MOE_TOPK].astype(jnp.int32).reshape(A)
    weight = route[:, MOE_TOPK:2 * MOE_TOPK].reshape(A)
    onehot = (expert[:, None] == jnp.arange(n_experts, dtype=jnp.int32)[None, :]).astype(jnp.int32)
    csum = jnp.cumsum(onehot, axis=0)
    rank = jnp.sum(onehot * csum, axis=1) - 1
    counts = csum[-1]
    tiles_per = (counts + tm - 1) // tm
    tile_end = jnp.cumsum(tiles_per)
    tile_start = tile_end - tiles_per
    dest = tile_start[expert] * tm + rank
    row_token = jnp.zeros((n_tiles * tm,), jnp.int32).at[dest].set(jnp.arange(A, dtype=jnp.int32) // MOE_TOPK)
    row_weight = jnp.zeros((n_tiles * tm,), F32).at[dest].set(weight)
    used = tile_end[-1]
    tix = jnp.arange(n_tiles, dtype=jnp.int32)
    tile_valid = (tix < used).astype(jnp.int32)
    clipped = jnp.minimum(tix, used - 1)
    tile_expert = jnp.sum((clipped[:, None] >= tile_end[None, :]).astype(jnp.int32), axis=1)
    tile_expert = jnp.minimum(tile_expert, n_experts - 1).astype(jnp.int32)
    dest2 = dest.reshape(N, MOE_TOPK)
    return tile_expert, tile_valid, row_token, row_weight, dest2[:, 0], dest2[:, 1]


def kernel(x, c, w_ada, b_ada, attn_norm, ffn_norm, w_qkv, w_o, qk_norm_q, qk_norm_k,
           ffn_w_in, ffn_w_out, router_w, moe_w_in, moe_w_out):
    B, S, D = x.shape
    depth = w_ada.shape[0]
    H = D // HEAD_DIM
    N = B * S
    mod = _adaln(c, w_ada, b_ada).reshape(depth, B, 6, 1, D)
    x2d = x.reshape(N, D)
    for i in range(depth):
        mod_l = mod[i]
        wqkv = w_qkv[i].astype(BF16)
        wo = w_o[i].astype(BF16)
        if i % 2 == 0:
            qkv = _qkv_proj(x2d, attn_norm[i], mod_l, wqkv, S)
            o = _stickbreak_attention(qkv.reshape(B, S, 3 * D), H)
            x2d, h = _outproj(o.reshape(N, D), wo, x2d, ffn_norm[i], mod_l, S, BF16)
            x2d = _dense_ffn(h, ffn_w_in[i // 2].astype(BF16), ffn_w_out[i // 2].astype(BF16),
                             x2d, mod_l, S)
        else:
            jj = i // 2
            gqk = jnp.concatenate([jnp.tile(qk_norm_q[jj], H), jnp.tile(qk_norm_k[jj], H),
                                   jnp.ones((D,), F32)]).reshape(1, 3 * D)
            qkv = _qkv_proj(x2d, attn_norm[i], mod_l, wqkv, S, gqk)
            o = _moba_attention(qkv.reshape(B, S, 3 * D), H)
            x2d, h, route = _outproj(o.reshape(N, D), wo, x2d, ffn_norm[i], mod_l, S, F32,
                                     router_w[jj])
            n_experts = router_w.shape[-1]
            tm = _tile(N * MOE_TOPK // n_experts, 512)
            te, tv, row_token, row_weight, pos0, pos1 = _routing_tables(route, n_experts, tm)
            y_sorted = _moe_grouped(h, moe_w_in[jj].astype(BF16), moe_w_out[jj].astype(BF16),
                                    te, tv, row_token, row_weight, tm)
            x2d = _moe_combine(y_sorted, pos0, pos1, x2d, mod_l, S)
    return x2d.reshape(B, S, D)
```

```python
import functools

import jax
import jax.numpy as jnp
from jax import lax
from jax.experimental import pallas as pl
from jax.experimental.pallas import tpu as pltpu

F32 = jnp.float32
BF16 = jnp.bfloat16

HEAD_DIM = 128
MOBA_BLOCK = 256
MOBA_TOPK = 3
MOE_TOPK = 2
EPS = 1e-6
MASKED = -1e30
LANES = 128
BF16_SUBLANES = 16
VMEM_LIMIT = 56 * 1024 * 1024


def _tile(dim, pref):
    t = min(dim, pref)
    while dim % t:
        t //= 2
    return t


def _params(*sem):
    return pltpu.CompilerParams(dimension_semantics=sem, vmem_limit_bytes=VMEM_LIMIT)


def _dot(a, b):
    return jnp.dot(a, b, preferred_element_type=F32)


def _dot_nt(a, b):
    return lax.dot_general(a, b, (((1,), (1,)), ((), ())), preferred_element_type=F32)


def _split_bf16(x):
    hi = x.astype(BF16)
    lo = (x - hi.astype(F32)).astype(BF16)
    return hi, lo


def _adaln_kernel(c_ref, w_ref, b_ref, o_ref):
    acc = _dot(c_ref[...].astype(BF16), w_ref[0].astype(BF16))
    o_ref[0] = acc + b_ref[0]


def _adaln(c, w_ada, b_ada):
    L, D, D6 = w_ada.shape
    B = c.shape[0]
    tn = _tile(D6, 1024)
    return pl.pallas_call(
        _adaln_kernel,
        grid=(L, D6 // tn),
        in_specs=[
            pl.BlockSpec((B, D), lambda l, j: (0, 0)),
            pl.BlockSpec((1, D, tn), lambda l, j: (l, 0, j)),
            pl.BlockSpec((1, 1, tn), lambda l, j: (l, 0, j)),
        ],
        out_specs=pl.BlockSpec((1, B, tn), lambda l, j: (l, 0, j)),
        out_shape=jax.ShapeDtypeStruct((L, B, D6), F32),
        compiler_params=_params("parallel", "parallel"),
        name="adaln",
    )(c, w_ada, b_ada.reshape(L, 1, D6))


def _norm_mod(x, gain, scale, shift):
    ms = jnp.mean(x * x, axis=-1, keepdims=True)
    y = x * lax.rsqrt(ms + EPS) * gain
    return y * (1.0 + scale) + shift


def _qkv_kernel(x_ref, g_ref, sc_ref, sh_ref, w_ref, *rest, qk_norm, d_model):
    if qk_norm:
        gqk_ref, o_ref, h_scr = rest
    else:
        o_ref, h_scr = rest
    j = pl.program_id(1)
    tn = o_ref.shape[1]

    @pl.when(j == 0)
    def _():
        h = _norm_mod(x_ref[...], g_ref[...], sc_ref[0, 0], sh_ref[0, 0])
        h_scr[...] = h.astype(BF16)

    acc = _dot(h_scr[...], w_ref[...])
    if not qk_norm:
        o_ref[...] = acc.astype(o_ref.dtype)
    else:
        is_qk = j * tn < 2 * d_model

        @pl.when(is_qk)
        def _():
            for hh in range(tn // HEAD_DIM):
                sl = slice(hh * HEAD_DIM, (hh + 1) * HEAD_DIM)
                blk = acc[:, sl]
                ms = jnp.mean(blk * blk, axis=-1, keepdims=True)
                o_ref[:, sl] = (blk * lax.rsqrt(ms + EPS) * gqk_ref[:, sl]).astype(o_ref.dtype)

        @pl.when(jnp.logical_not(is_qk))
        def _():
            o_ref[...] = acc.astype(o_ref.dtype)


def _qkv_proj(x2d, gain, mod_l, w_bf16, seq, gqk=None):
    N, D = x2d.shape
    D3 = w_bf16.shape[1]
    tm = _tile(seq, 512)
    tn = _tile(D, 1024)
    assert (2 * D) % tn == 0 and tn % HEAD_DIM == 0
    rows_per_batch = seq // tm
    qk_norm = gqk is not None
    in_specs = [
        pl.BlockSpec((tm, D), lambda i, j: (i, 0)),
        pl.BlockSpec((1, D), lambda i, j: (0, 0)),
        pl.BlockSpec((1, 1, 1, D), lambda i, j: (i // rows_per_batch, 1, 0, 0)),
        pl.BlockSpec((1, 1, 1, D), lambda i, j: (i // rows_per_batch, 0, 0, 0)),
        pl.BlockSpec((D, tn), lambda i, j: (0, j)),
    ]
    args = [x2d, gain.reshape(1, D), mod_l, mod_l, w_bf16]
    if qk_norm:
        in_specs.append(pl.BlockSpec((1, tn), lambda i, j: (0, j)))
        args.append(gqk)
    return pl.pallas_call(
        functools.partial(_qkv_kernel, qk_norm=qk_norm, d_model=D),
        grid=(N // tm, D3 // tn),
        in_specs=in_specs,
        out_specs=pl.BlockSpec((tm, tn), lambda i, j: (i, j)),
        out_shape=jax.ShapeDtypeStruct((N, D3), BF16),
        scratch_shapes=[pltpu.VMEM((tm, D), BF16)],
        compiler_params=_params("parallel", "arbitrary"),
        name="qkv_norm" if qk_norm else "qkv",
    )(*args)


def _dot_tn(a, b):
    return lax.dot_general(a, b, (((0,), (0,)), ((), ())), preferred_element_type=F32)


def _sb_kernel(q_ref, k_ref, v_ref, o_ref, *, scale, tq):
    n_qblocks = q_ref.shape[1] // tq
    key = lax.broadcasted_iota(jnp.int32, (tq, tq), 0)
    qry = lax.broadcasted_iota(jnp.int32, (tq, tq), 1)
    past = key < qry
    later_keys = jnp.where(qry > key, 1.0, 0.0).astype(BF16)

    def rows(ref, i):
        return ref[0, i * tq:(i + 1) * tq, :]

    def score_stage(qi):
        q = rows(q_ref, qi)
        out = []
        for j in range(qi + 1):
            z = _dot_nt(rows(k_ref, j), q) * scale
            log_beta = jnp.minimum(z, 0.0) - jnp.log(1.0 + jnp.exp(-jnp.abs(z)))
            log_keep = log_beta - z
            if j == qi:
                log_keep = jnp.where(past, log_keep, 0.0)
            out.append((log_beta, log_keep) + _split_bf16(log_keep))
        return out

    def cumsum_stage(parts):
        return [_dot(later_keys, hi) + _dot(later_keys, lo) for (_, _, hi, lo) in parts]

    def value_stage(qi, parts, laters):
        carry = jnp.zeros((1, tq), F32)
        acc = jnp.zeros((HEAD_DIM, tq), F32)
        for j in range(qi, -1, -1):
            log_beta, log_keep, _, _ = parts[j]
            a = jnp.exp(log_beta + laters[j] + carry)
            if j == qi:
                a = jnp.where(past, a, 0.0)
            acc = acc + _dot_tn(rows(v_ref, j), a.astype(BF16))
            carry = carry + laters[j][0:1, :] + log_keep[0:1, :]
        o_ref[0, qi * tq:(qi + 1) * tq, :] = acc.T.astype(o_ref.dtype)

    parts = score_stage(0)
    for qi in range(n_qblocks):
        laters = cumsum_stage(parts)
        nxt = score_stage(qi + 1) if qi + 1 < n_qblocks else None
        value_stage(qi, parts, laters)
        parts = nxt


def _head_specs(S, H):
    spec = lambda col: pl.BlockSpec((1, S, HEAD_DIM), lambda b, h: (b, 0, col * H + h))
    return [spec(0), spec(1), spec(2)], spec(0)


def _stickbreak_attention(qkv, n_heads):
    B, S, _ = qkv.shape
    H = n_heads
    in_specs, out_spec = _head_specs(S, H)
    return pl.pallas_call(
        functools.partial(_sb_kernel, scale=HEAD_DIM ** -0.5, tq=_tile(S, 256)),
        grid=(B, H),
        in_specs=in_specs,
        out_specs=out_spec,
        out_shape=jax.ShapeDtypeStruct((B, S, H * HEAD_DIM), BF16),
        compiler_params=_params("parallel", "parallel"),
        name="stickbreak_attn",
    )(qkv, qkv, qkv)


def _moba_kernel(q_ref, k_ref, v_ref, o_ref, kmean_scr, *, scale, n_heads):
    h = pl.program_id(1)
    T = MOBA_BLOCK
    n_blocks = q_ref.shape[1] // T

    def rows(ref, i):
        return ref[0, i * T:(i + 1) * T, :]

    kmean_scr[...] = jnp.zeros_like(kmean_scr)
    for n in range(n_blocks):
        kmean_scr[n:n + 1, :] = jnp.mean(rows(k_ref, n).astype(F32), axis=0, keepdims=True)
    km_hi, km_lo = _split_bf16(kmean_scr[...])
    blk = lax.broadcasted_iota(jnp.int32, (kmean_scr.shape[0], T), 0)
    slope = jnp.exp2(jnp.full((1, 1), h + 1, jnp.int32).astype(F32) * (-8.0 / n_heads))
    rel = (lax.broadcasted_iota(jnp.int32, (T, T), 1)
           - lax.broadcasted_iota(jnp.int32, (T, T), 0)).astype(F32)

    def choose(qi, q):
        gate = _dot_nt(km_hi, q) + _dot_nt(km_lo, q)
        fully_past = blk < qi
        g = jnp.where(fully_past, gate, -jnp.inf)
        beaten = jnp.zeros(gate.shape, F32)
        for m in range(qi):
            gm = g[m:m + 1, :]
            first = jnp.where(blk > m, 1.0, 0.0)
            beaten = beaten + jnp.where(gm > g, 1.0, jnp.where(gm == g, first, 0.0))
        return jnp.where(fully_past, jnp.where(beaten < MOBA_TOPK, 1.0, 0.0), 0.0)

    def score_stage(qi):
        q = rows(q_ref, qi)
        chosen = choose(qi, q)
        scores = []
        m_run = None
        for j in range(qi + 1):
            dist = rel + float((qi - j) * T)
            s = _dot_nt(rows(k_ref, j), q) * scale - slope * dist
            if j == qi:
                s = jnp.where(dist >= 0.0, s, MASKED)
            else:
                s = jnp.where(chosen[j:j + 1, :] > 0.0, s, MASKED)
            scores.append(s)
            m_j = jnp.max(s, axis=0, keepdims=True)
            m_run = m_j if m_run is None else jnp.maximum(m_run, m_j)
        return scores, m_run

    def value_stage(qi, scores, m_run):
        l_run = jnp.zeros((1, T), F32)
        acc = jnp.zeros((HEAD_DIM, T), F32)
        for j in range(qi + 1):
            p = jnp.exp(scores[j] - m_run)
            l_run = l_run + jnp.sum(p, axis=0, keepdims=True)
            acc = acc + _dot_tn(rows(v_ref, j), p.astype(BF16))
        o_ref[0, qi * T:(qi + 1) * T, :] = (acc / l_run).T.astype(o_ref.dtype)

    cur = score_stage(0)
    for qi in range(n_blocks):
        nxt = score_stage(qi + 1) if qi + 1 < n_blocks else None
        value_stage(qi, *cur)
        cur = nxt


def _moba_attention(qkv, n_heads):
    B, S, _ = qkv.shape
    H = n_heads
    assert S % MOBA_BLOCK == 0
    nb = S // MOBA_BLOCK
    gate_rows = -(-nb // BF16_SUBLANES) * BF16_SUBLANES
    in_specs, out_spec = _head_specs(S, H)
    return pl.pallas_call(
        functools.partial(_moba_kernel, scale=HEAD_DIM ** -0.5, n_heads=H),
        grid=(B, H),
        in_specs=in_specs,
        out_specs=out_spec,
        out_shape=jax.ShapeDtypeStruct((B, S, H * HEAD_DIM), BF16),
        scratch_shapes=[pltpu.VMEM((gate_rows, HEAD_DIM), F32)],
        compiler_params=_params("parallel", "parallel"),
        name="moba_attn",
    )(qkv, qkv, qkv)


def _outproj_kernel(o_ref, w_ref, x_ref, ga_ref, gn_ref, sc_ref, sh_ref, *rest, n_experts):
    if n_experts:
        wr_hi_ref, wr_lo_ref, x1_ref, h_ref, route_ref = rest
    else:
        x1_ref, h_ref = rest
    x1 = x_ref[...] + ga_ref[0, 0] * _dot(o_ref[...], w_ref[...])
    x1_ref[...] = x1
    h = _norm_mod(x1, gn_ref[...], sc_ref[0, 0], sh_ref[0, 0])
    h_ref[...] = h.astype(h_ref.dtype)
    if n_experts:
        h_hi, h_lo = _split_bf16(h)
        logits = (_dot(h_hi, wr_hi_ref[...]) + _dot(h_hi, wr_lo_ref[...])
                  + _dot(h_lo, wr_hi_ref[...]))
        lane = lax.broadcasted_iota(jnp.int32, logits.shape, 1).astype(F32)
        lg = jnp.where(lane < n_experts, logits, -jnp.inf)
        v1 = jnp.max(lg, axis=-1, keepdims=True)
        i1 = jnp.min(jnp.where(lg == v1, lane, float(LANES)), axis=-1, keepdims=True)
        lg2 = jnp.where(lane == i1, -jnp.inf, lg)
        v2 = jnp.max(lg2, axis=-1, keepdims=True)
        i2 = jnp.min(jnp.where(lg2 == v2, lane, float(LANES)), axis=-1, keepdims=True)
        e2 = jnp.exp(v2 - v1)
        w1 = 1.0 / (1.0 + e2)
        w2 = e2 / (1.0 + e2)
        route_ref[...] = jnp.where(
            lane == 0.0, i1,
            jnp.where(lane == 1.0, i2, jnp.where(lane == 2.0, w1, jnp.where(lane == 3.0, w2, 0.0))))


def _outproj(o2d, w_bf16, x2d, gain, mod_l, seq, h_dtype, router_w=None):
    N, D = x2d.shape
    tm = _tile(seq, 256)
    rows_per_batch = seq // tm
    mod_spec = lambda k: pl.BlockSpec((1, 1, 1, D), lambda i: (i // rows_per_batch, k, 0, 0))
    row_spec = pl.BlockSpec((tm, D), lambda i: (i, 0))
    in_specs = [
        row_spec,
        pl.BlockSpec((D, D), lambda i: (0, 0)),
        row_spec,
        mod_spec(2), pl.BlockSpec((1, D), lambda i: (0, 0)), mod_spec(4), mod_spec(3),
    ]
    args = [o2d, w_bf16, x2d, mod_l, gain.reshape(1, D), mod_l, mod_l]
    out_specs = [row_spec, row_spec]
    out_shape = [jax.ShapeDtypeStruct((N, D), F32), jax.ShapeDtypeStruct((N, D), h_dtype)]
    n_experts = 0
    if router_w is not None:
        n_experts = router_w.shape[1]
        assert n_experts <= LANES
        wr = jnp.zeros((D, LANES), F32).at[:, :n_experts].set(router_w)
        wr_hi = wr.astype(BF16)
        wr_lo = (wr - wr_hi.astype(F32)).astype(BF16)
        in_specs += [pl.BlockSpec((D, LANES), lambda i: (0, 0))] * 2
        args += [wr_hi, wr_lo]
        out_specs.append(pl.BlockSpec((tm, LANES), lambda i: (i, 0)))
        out_shape.append(jax.ShapeDtypeStruct((N, LANES), F32))
    return pl.pallas_call(
        functools.partial(_outproj_kernel, n_experts=n_experts),
        grid=(N // tm,),
        in_specs=in_specs,
        out_specs=out_specs,
        out_shape=out_shape,
        compiler_params=_params("parallel"),
        name="outproj_route" if n_experts else "outproj",
    )(*args)


def _swiglu_act(h, wg, wu):
    g = _dot(h, wg)
    u = _dot(h, wu)
    return (g * (1.0 / (1.0 + jnp.exp(-g))) * u).astype(BF16)


def _ffn_kernel(h_ref, wg_ref, wu_ref, wo_ref, x_ref, gf_ref, out_ref, acc_scr):
    j = pl.program_id(1)

    @pl.when(j == 0)
    def _():
        acc_scr[...] = jnp.zeros_like(acc_scr)

    acc_scr[...] += _dot(_swiglu_act(h_ref[...], wg_ref[...], wu_ref[...]), wo_ref[...])

    @pl.when(j == pl.num_programs(1) - 1)
    def _():
        out_ref[...] = x_ref[...] + gf_ref[0, 0] * acc_scr[...]


def _dense_ffn(h_bf16, w_in_bf16, w_out_bf16, x2d, mod_l, seq):
    N, D = x2d.shape
    d_ff = w_out_bf16.shape[0]
    tm = _tile(seq, 512)
    tn = _tile(d_ff, 512)
    nj = d_ff // tn
    rows_per_batch = seq // tm
    row_spec = pl.BlockSpec((tm, D), lambda i, j: (i, 0))
    return pl.pallas_call(
        _ffn_kernel,
        grid=(N // tm, nj),
        in_specs=[
            row_spec,
            pl.BlockSpec((D, tn), lambda i, j: (0, j)),
            pl.BlockSpec((D, tn), lambda i, j: (0, j + nj)),
            pl.BlockSpec((tn, D), lambda i, j: (j, 0)),
            row_spec,
            pl.BlockSpec((1, 1, 1, D), lambda i, j: (i // rows_per_batch, 5, 0, 0)),
        ],
        out_specs=row_spec,
        out_shape=jax.ShapeDtypeStruct((N, D), F32),
        scratch_shapes=[pltpu.VMEM((tm, D), F32)],
        compiler_params=_params("parallel", "arbitrary"),
        name="dense_ffn",
    )(h_bf16, w_in_bf16, w_in_bf16, w_out_bf16, x2d, mod_l)


DMA_ISSUE_UNROLL = 8


def _moe_kernel(te_ref, nv_ref, rt_ref, h_hbm, wg_ref, wu_ref, wo_ref, y_hbm,
                rows_scr, stage_scr, wg_scr, wu_scr, wo_scr, acc_scr, sem, out_sem, *, sub):
    t = pl.program_id(0)
    j = pl.program_id(1)
    n_valid = nv_ref[t]
    tile_rows = rows_scr.shape[0]
    chunk = stage_scr.shape[1]
    n_chunks = tile_rows // chunk
    valid_chunks = n_valid * (sub // chunk)

    def out_copy(tile):
        return pltpu.make_async_copy(acc_scr, y_hbm.at[pl.ds(tile * tile_rows, tile_rows), :], out_sem)

    def row_copy(s, r):
        tok = rt_ref[t * tile_rows + s * chunk + r]
        return pltpu.make_async_copy(h_hbm.at[pl.ds(tok, 1), :],
                                     stage_scr.at[s % 2, pl.ds(r, 1), :], sem.at[s % 2])

    def issue(s):
        def body(r, c):
            row_copy(s, r).start()
            return c
        lax.fori_loop(0, chunk, body, 0, unroll=DMA_ISSUE_UNROLL)

    def drain(s):
        def body(r, c):
            row_copy(s, r).wait()
            return c
        lax.fori_loop(0, chunk, body, 0, unroll=DMA_ISSUE_UNROLL)
        rows_scr[s * chunk:(s + 1) * chunk, :] = stage_scr[s % 2].astype(BF16)

    @pl.when(j == 0)
    def _():
        pl.when(valid_chunks > 0)(functools.partial(issue, 0))
        for s in range(n_chunks):
            if s + 1 < n_chunks:
                pl.when(s + 1 < valid_chunks)(functools.partial(issue, s + 1))
            pl.when(s < valid_chunks)(functools.partial(drain, s))
        pl.when(t > 0)(lambda: out_copy(t - 1).wait())
        acc_scr[...] = jnp.zeros_like(acc_scr)

    @pl.when(n_valid > 0)
    def _():
        wg_scr[...] = wg_ref[0].astype(BF16)
        wu_scr[...] = wu_ref[0].astype(BF16)
        wo_scr[...] = wo_ref[0].astype(BF16)

    def compute(s, c):
        r0 = pl.multiple_of(s * sub, sub)
        act = _swiglu_act(rows_scr[pl.ds(r0, sub), :], wg_scr[...], wu_scr[...])
        acc_scr[pl.ds(r0, sub), :] += _dot(act, wo_scr[...])
        return c

    lax.fori_loop(0, n_valid, compute, 0)

    @pl.when(j == pl.num_programs(1) - 1)
    def _():
        out_copy(t).start()
        pl.when(t == pl.num_programs(0) - 1)(lambda: out_copy(t).wait())


def _moe_grouped(h_f32, w_in, w_out, tile_expert, tile_nvalid, row_token, tile_rows, sub):
    N, D = h_f32.shape
    E, d_ff, _ = w_out.shape
    n_tiles = tile_expert.shape[0]
    tn = _tile(d_ff, 256)
    nj = d_ff // tn

    def ff_index(t, j, nv):
        return jnp.where(nv[t] > 0, j, nj - 1)

    chunk = sub // 2 if sub % 16 == 0 else sub
    return pl.pallas_call(
        functools.partial(_moe_kernel, sub=sub),
        grid_spec=pltpu.PrefetchScalarGridSpec(
            num_scalar_prefetch=3,
            grid=(n_tiles, nj),
            in_specs=[
                pl.BlockSpec(memory_space=pl.ANY),
                pl.BlockSpec((1, D, tn), lambda t, j, te, nv, rt: (te[t], 0, ff_index(t, j, nv))),
                pl.BlockSpec((1, D, tn), lambda t, j, te, nv, rt: (te[t], 0, ff_index(t, j, nv) + nj)),
                pl.BlockSpec((1, tn, D), lambda t, j, te, nv, rt: (te[t], ff_index(t, j, nv), 0)),
            ],
            out_specs=pl.BlockSpec(memory_space=pl.ANY),
            scratch_shapes=[
                pltpu.VMEM((tile_rows, D), BF16),
                pltpu.VMEM((2, chunk, D), F32),
                pltpu.VMEM((D, tn), BF16),
                pltpu.VMEM((D, tn), BF16),
                pltpu.VMEM((tn, D), BF16),
                pltpu.VMEM((tile_rows, D), F32),
                pltpu.SemaphoreType.DMA((2,)),
                pltpu.SemaphoreType.DMA(()),
            ],
        ),
        out_shape=jax.ShapeDtypeStruct((n_tiles * tile_rows, D), F32),
        compiler_params=_params("arbitrary", "arbitrary"),
        name="moe_grouped",
    )(tile_expert, tile_nvalid, row_token, h_f32, w_in, w_in, w_out)


def _combine_kernel(p0_ref, p1_ref, ys_hbm, x_ref, gf_ref, route_ref, out_ref, buf0, buf1, sem0, sem1):
    i = pl.program_id(0)
    tm = buf0.shape[0]

    def copies(r):
        c0 = pltpu.make_async_copy(ys_hbm.at[pl.ds(p0_ref[i * tm + r], 1), :],
                                   buf0.at[pl.ds(r, 1), :], sem0)
        c1 = pltpu.make_async_copy(ys_hbm.at[pl.ds(p1_ref[i * tm + r], 1), :],
                                   buf1.at[pl.ds(r, 1), :], sem1)
        return c0, c1

    def start(r, c):
        c0, c1 = copies(r)
        c0.start()
        c1.start()
        return c

    lax.fori_loop(0, tm, start, 0, unroll=DMA_ISSUE_UNROLL)

    def wait(r, c):
        c0, c1 = copies(r)
        c0.wait()
        c1.wait()
        return c

    lax.fori_loop(0, tm, wait, 0, unroll=DMA_ISSUE_UNROLL)
    w0 = route_ref[:, MOE_TOPK:MOE_TOPK + 1]
    w1 = route_ref[:, MOE_TOPK + 1:MOE_TOPK + 2]
    out_ref[...] = x_ref[...] + gf_ref[0, 0] * (w0 * buf0[...] + w1 * buf1[...])


def _moe_combine(y_sorted, pos0, pos1, x2d, mod_l, route, seq):
    N, D = x2d.shape
    tm = _tile(seq, 256)
    rows_per_batch = seq // tm
    return pl.pallas_call(
        _combine_kernel,
        grid_spec=pltpu.PrefetchScalarGridSpec(
            num_scalar_prefetch=2,
            grid=(N // tm,),
            in_specs=[
                pl.BlockSpec(memory_space=pl.ANY),
                pl.BlockSpec((tm, D), lambda i, p0, p1: (i, 0)),
                pl.BlockSpec((1, 1, 1, D), lambda i, p0, p1: (i // rows_per_batch, 5, 0, 0)),
                pl.BlockSpec((tm, LANES), lambda i, p0, p1: (i, 0)),
            ],
            out_specs=pl.BlockSpec((tm, D), lambda i, p0, p1: (i, 0)),
            scratch_shapes=[
                pltpu.VMEM((tm, D), F32),
                pltpu.VMEM((tm, D), F32),
                pltpu.SemaphoreType.DMA(()),
                pltpu.SemaphoreType.DMA(()),
            ],
        ),
        out_shape=jax.ShapeDtypeStruct((N, D), F32),
        compiler_params=_params("arbitrary"),
        name="moe_combine",
    )(pos0, pos1, y_sorted, x2d, mod_l, route)


def _moe_tiling(n_assign, n_experts):
    per_expert = n_assign // n_experts
    sub = _tile(per_expert, 512)
    tile_rows = sub * (-(-(per_expert // 2 + sub) // sub))
    n_tiles = n_assign // tile_rows + n_experts
    return sub, tile_rows, n_tiles


def _routing_tables(route, n_experts, sub, tile_rows, n_tiles):
    N = route.shape[0]
    A = N * MOE_TOPK
    expert = route[:, :MOE_TOPK].astype(jnp.int32).reshape(A)
    onehot = (expert[:, None] == jnp.arange(n_experts, dtype=jnp.int32)[None, :]).astype(jnp.int32)
    csum = jnp.cumsum(onehot, axis=0)
    rank = jnp.sum(onehot * csum, axis=1) - 1
    counts = csum[-1]
    tiles_per = (counts + tile_rows - 1) // tile_rows
    tile_end = jnp.cumsum(tiles_per)
    tile_start = tile_end - tiles_per
    dest = tile_start[expert] * tile_rows + rank
    row_token = jnp.zeros((n_tiles * tile_rows,), jnp.int32).at[dest].set(
        jnp.arange(A, dtype=jnp.int32) // MOE_TOPK)
    used = tile_end[-1]
    tix = jnp.arange(n_tiles, dtype=jnp.int32)
    clipped = jnp.minimum(tix, used - 1)
    tile_expert = jnp.sum((clipped[:, None] >= tile_end[None, :]).astype(jnp.int32), axis=1)
    tile_expert = jnp.minimum(tile_expert, n_experts - 1).astype(jnp.int32)
    rows_left = counts[tile_expert] - (tix - tile_start[tile_expert]) * tile_rows
    n_valid = jnp.clip((rows_left + sub - 1) // sub, 0, tile_rows // sub)
    tile_nvalid = jnp.where(tix < used, n_valid, 0).astype(jnp.int32)
    dest2 = dest.reshape(N, MOE_TOPK)
    return tile_expert, tile_nvalid, row_token, dest2[:, 0], dest2[:, 1]


def kernel(x, c, w_ada, b_ada, attn_norm, ffn_norm, w_qkv, w_o, qk_norm_q, qk_norm_k,
           ffn_w_in, ffn_w_out, router_w, moe_w_in, moe_w_out):
    B, S, D = x.shape
    depth = w_ada.shape[0]
    H = D // HEAD_DIM
    N = B * S
    mod = _adaln(c, w_ada, b_ada).reshape(depth, B, 6, 1, D)
    x2d = x.reshape(N, D)
    for i in range(depth):
        mod_l = mod[i]
        wqkv = w_qkv[i].astype(BF16)
        wo = w_o[i].astype(BF16)
        if i % 2 == 0:
            qkv = _qkv_proj(x2d, attn_norm[i], mod_l, wqkv, S)
            o = _stickbreak_attention(qkv.reshape(B, S, 3 * D), H)
            x2d, h = _outproj(o.reshape(N, D), wo, x2d, ffn_norm[i], mod_l, S, BF16)
            x2d = _dense_ffn(h, ffn_w_in[i // 2].astype(BF16), ffn_w_out[i // 2].astype(BF16),
                             x2d, mod_l, S)
        else:
            jj = i // 2
            gqk = jnp.concatenate([jnp.tile(qk_norm_q[jj], H), jnp.tile(qk_norm_k[jj], H),
                                   jnp.ones((D,), F32)]).reshape(1, 3 * D)
            qkv = _qkv_proj(x2d, attn_norm[i], mod_l, wqkv, S, gqk)
            o = _moba_attention(qkv.reshape(B, S, 3 * D), H)
            x2d, h, route = _outproj(o.reshape(N, D), wo, x2d, ffn_norm[i], mod_l, S, F32,
                                     router_w[jj])
            n_experts = router_w.shape[-1]
            sub, tile_rows, n_tiles = _moe_tiling(N * MOE_TOPK, n_experts)
            te, nv, row_token, pos0, pos1 = _routing_tables(route, n_experts, sub, tile_rows, n_tiles)
            y_sorted = _moe_grouped(h, moe_w_in[jj], moe_w_out[jj], te, nv, row_token, tile_rows, sub)
            x2d = _moe_combine(y_sorted, pos0, pos1, x2d, mod_l, route, S)
    return x2d.reshape(B, S, D)
```

```python
import functools

import jax
import jax.numpy as jnp
from jax import lax
from jax.experimental import pallas as pl
from jax.experimental.pallas import tpu as pltpu

F32 = jnp.float32
BF16 = jnp.bfloat16

HEAD_DIM = 128
MOBA_BLOCK = 256
MOBA_TOPK = 3
MOE_TOPK = 2
EPS = 1e-6
MASKED = -1e30
LANES = 128
BF16_SUBLANES = 16
VMEM_LIMIT = 56 * 1024 * 1024


def _tile(dim, pref):
    t = min(dim, pref)
    while dim % t:
        t //= 2
    return t


def _params(*sem):
    return pltpu.CompilerParams(dimension_semantics=sem, vmem_limit_bytes=VMEM_LIMIT)


def _dot(a, b):
    return jnp.dot(a, b, preferred_element_type=F32)


def _dot_nt(a, b):
    return lax.dot_general(a, b, (((1,), (1,)), ((), ())), preferred_element_type=F32)


def _split_bf16(x):
    hi = x.astype(BF16)
    lo = (x - hi.astype(F32)).astype(BF16)
    return hi, lo


def _adaln_kernel(c_ref, w_ref, b_ref, o_ref):
    acc = _dot(c_ref[...].astype(BF16), w_ref[0].astype(BF16))
    o_ref[0] = acc + b_ref[0]


def _adaln(c, w_ada, b_ada):
    L, D, D6 = w_ada.shape
    B = c.shape[0]
    tn = _tile(D6, 1024)
    return pl.pallas_call(
        _adaln_kernel,
        grid=(L, D6 // tn),
        in_specs=[
            pl.BlockSpec((B, D), lambda l, j: (0, 0)),
            pl.BlockSpec((1, D, tn), lambda l, j: (l, 0, j)),
            pl.BlockSpec((1, 1, tn), lambda l, j: (l, 0, j)),
        ],
        out_specs=pl.BlockSpec((1, B, tn), lambda l, j: (l, 0, j)),
        out_shape=jax.ShapeDtypeStruct((L, B, D6), F32),
        compiler_params=_params("parallel", "parallel"),
        name="adaln",
    )(c, w_ada, b_ada.reshape(L, 1, D6))


def _norm_mod(x, gain, scale, shift):
    ms = jnp.mean(x * x, axis=-1, keepdims=True)
    y = x * lax.rsqrt(ms + EPS) * gain
    return y * (1.0 + scale) + shift


def _qkv_kernel(x_ref, g_ref, sc_ref, sh_ref, w_ref, col_ref, o_ref, h_scr, *, qk_norm, d_model):
    j = pl.program_id(1)
    tn = o_ref.shape[1]

    @pl.when(j == 0)
    def _():
        h = _norm_mod(x_ref[...], g_ref[...], sc_ref[0, 0], sh_ref[0, 0])
        h_scr[...] = h.astype(BF16)

    acc = _dot(h_scr[...], w_ref[...])
    if not qk_norm:
        o_ref[...] = (acc * col_ref[...]).astype(o_ref.dtype)
    else:
        is_qk = j * tn < 2 * d_model

        @pl.when(is_qk)
        def _():
            for hh in range(tn // HEAD_DIM):
                sl = slice(hh * HEAD_DIM, (hh + 1) * HEAD_DIM)
                blk = acc[:, sl]
                ms = jnp.mean(blk * blk, axis=-1, keepdims=True)
                o_ref[:, sl] = (blk * lax.rsqrt(ms + EPS) * col_ref[:, sl]).astype(o_ref.dtype)

        @pl.when(jnp.logical_not(is_qk))
        def _():
            o_ref[...] = acc.astype(o_ref.dtype)


Q_FOLD = HEAD_DIM ** -0.5 * 1.4426950408889634


def _qkv_proj(x2d, gain, mod_l, w_bf16, seq, n_heads, qk_gains=None):
    N, D = x2d.shape
    D3 = w_bf16.shape[1]
    tm = _tile(seq, 1024)
    tn = _tile(D, 1024)
    assert (2 * D) % tn == 0 and tn % HEAD_DIM == 0
    rows_per_batch = seq // tm
    qk_norm = qk_gains is not None
    gq, gk = qk_gains if qk_norm else (jnp.ones((HEAD_DIM,), F32),) * 2
    col = jnp.concatenate([jnp.tile(gq, n_heads) * Q_FOLD, jnp.tile(gk, n_heads),
                           jnp.ones((D,), F32)]).reshape(1, D3)
    in_specs = [
        pl.BlockSpec((tm, D), lambda i, j: (i, 0)),
        pl.BlockSpec((1, D), lambda i, j: (0, 0)),
        pl.BlockSpec((1, 1, 1, D), lambda i, j: (i // rows_per_batch, 1, 0, 0)),
        pl.BlockSpec((1, 1, 1, D), lambda i, j: (i // rows_per_batch, 0, 0, 0)),
        pl.BlockSpec((D, tn), lambda i, j: (0, j)),
        pl.BlockSpec((1, tn), lambda i, j: (0, j)),
    ]
    args = [x2d, gain.reshape(1, D), mod_l, mod_l, w_bf16, col]
    return pl.pallas_call(
        functools.partial(_qkv_kernel, qk_norm=qk_norm, d_model=D),
        grid=(N // tm, D3 // tn),
        in_specs=in_specs,
        out_specs=pl.BlockSpec((tm, tn), lambda i, j: (i, j)),
        out_shape=jax.ShapeDtypeStruct((N, D3), BF16),
        scratch_shapes=[pltpu.VMEM((tm, D), BF16)],
        compiler_params=_params("parallel", "arbitrary"),
        name="qkv_norm" if qk_norm else "qkv",
    )(*args)


def _dot_tn(a, b):
    return lax.dot_general(a, b, (((0,), (0,)), ((), ())), preferred_element_type=F32)


def _sb_kernel(q_ref, k_ref, v_ref, o_ref, *, tq):
    n_qblocks = q_ref.shape[1] // tq
    key = lax.broadcasted_iota(jnp.int32, (tq, tq), 0)
    qry = lax.broadcasted_iota(jnp.int32, (tq, tq), 1)
    past = key < qry
    later_keys = jnp.where(qry > key, 1.0, 0.0).astype(BF16)

    def rows(ref, i):
        return ref[0, i * tq:(i + 1) * tq, :]

    def score_stage(qi):
        q = rows(q_ref, qi)
        out = []
        for j in range(qi + 1):
            z = _dot_nt(rows(k_ref, j), q)
            log_beta = jnp.minimum(z, 0.0) - jnp.log2(1.0 + jnp.exp2(-jnp.abs(z)))
            log_keep = log_beta - z
            if j == qi:
                log_keep = jnp.where(past, log_keep, 0.0)
            out.append((log_beta, log_keep))
        return out

    def cumsum_stage(parts):
        return [_dot(later_keys, log_keep.astype(BF16)) for (_, log_keep) in parts]

    def value_stage(qi, parts, laters):
        carry = jnp.zeros((1, tq), F32)
        acc = jnp.zeros((HEAD_DIM, tq), F32)
        for j in range(qi, -1, -1):
            log_beta, log_keep = parts[j]
            a = jnp.exp2(log_beta + laters[j] + carry)
            if j == qi:
                a = jnp.where(past, a, 0.0)
            acc = acc + _dot_tn(rows(v_ref, j), a.astype(BF16))
            carry = carry + laters[j][0:1, :] + log_keep[0:1, :]
        o_ref[0, qi * tq:(qi + 1) * tq, :] = acc.T.astype(o_ref.dtype)

    parts = score_stage(0)
    for qi in range(n_qblocks):
        laters = cumsum_stage(parts)
        nxt = score_stage(qi + 1) if qi + 1 < n_qblocks else None
        value_stage(qi, parts, laters)
        parts = nxt


def _head_specs(S, H):
    spec = lambda col: pl.BlockSpec((1, S, HEAD_DIM), lambda b, h: (b, 0, col * H + h))
    return [spec(0), spec(1), spec(2)], spec(0)


def _stickbreak_attention(qkv, n_heads):
    B, S, _ = qkv.shape
    H = n_heads
    in_specs, out_spec = _head_specs(S, H)
    return pl.pallas_call(
        functools.partial(_sb_kernel, tq=_tile(S, 256)),
        grid=(B, H),
        in_specs=in_specs,
        out_specs=out_spec,
        out_shape=jax.ShapeDtypeStruct((B, S, H * HEAD_DIM), BF16),
        compiler_params=_params("parallel", "parallel"),
        name="stickbreak_attn",
    )(qkv, qkv, qkv)


def _moba_kernel(q_ref, k_ref, v_ref, o_ref, kmean_scr, *, n_heads):
    h = pl.program_id(1)
    T = MOBA_BLOCK
    n_blocks = q_ref.shape[1] // T

    def rows(ref, i):
        return ref[0, i * T:(i + 1) * T, :]

    kmean_scr[...] = jnp.zeros_like(kmean_scr)
    for n in range(n_blocks):
        kmean_scr[n:n + 1, :] = jnp.mean(rows(k_ref, n).astype(F32), axis=0, keepdims=True)
    km_hi, km_lo = _split_bf16(kmean_scr[...])
    blk = lax.broadcasted_iota(jnp.int32, (kmean_scr.shape[0], T), 0)
    slope = jnp.exp2(jnp.full((1, 1), h + 1, jnp.int32).astype(F32) * (-8.0 / n_heads))
    slope2 = slope * 1.4426950408889634
    rel = (lax.broadcasted_iota(jnp.int32, (T, T), 1)
           - lax.broadcasted_iota(jnp.int32, (T, T), 0)).astype(F32)
    alibi = [slope2 * (rel + float(d * T)) for d in range(n_blocks)]

    def choose(qi, q):
        gate = _dot_nt(km_hi, q) + _dot_nt(km_lo, q)
        fully_past = blk < qi
        g = jnp.where(fully_past, gate, -jnp.inf)
        beaten = jnp.zeros(gate.shape, F32)
        for m in range(qi):
            gm = g[m:m + 1, :]
            first = jnp.where(blk > m, 1.0, 0.0)
            beaten = beaten + jnp.where(gm > g, 1.0, jnp.where(gm == g, first, 0.0))
        return jnp.where(fully_past, jnp.where(beaten < MOBA_TOPK, 1.0, 0.0), 0.0)

    def score_stage(qi):
        q = rows(q_ref, qi)
        chosen = choose(qi, q)
        scores = []
        m_run = None
        for j in range(qi + 1):
            s = _dot_nt(rows(k_ref, j), q) - alibi[qi - j]
            if j == qi:
                s = jnp.where(rel >= 0.0, s, MASKED)
            else:
                s = jnp.where(chosen[j:j + 1, :] > 0.0, s, MASKED)
            scores.append(s)
            m_j = jnp.max(s, axis=0, keepdims=True)
            m_run = m_j if m_run is None else jnp.maximum(m_run, m_j)
        return scores, m_run

    def value_stage(qi, scores, m_run):
        l_run = jnp.zeros((1, T), F32)
        acc = jnp.zeros((HEAD_DIM, T), F32)
        for j in range(qi + 1):
            p = jnp.exp2(scores[j] - m_run)
            l_run = l_run + jnp.sum(p, axis=0, keepdims=True)
            acc = acc + _dot_tn(rows(v_ref, j), p.astype(BF16))
        o_ref[0, qi * T:(qi + 1) * T, :] = (acc / l_run).T.astype(o_ref.dtype)

    cur = score_stage(0)
    for qi in range(n_blocks):
        nxt = score_stage(qi + 1) if qi + 1 < n_blocks else None
        value_stage(qi, *cur)
        cur = nxt


def _moba_attention(qkv, n_heads):
    B, S, _ = qkv.shape
    H = n_heads
    assert S % MOBA_BLOCK == 0
    nb = S // MOBA_BLOCK
    gate_rows = -(-nb // BF16_SUBLANES) * BF16_SUBLANES
    in_specs, out_spec = _head_specs(S, H)
    return pl.pallas_call(
        functools.partial(_moba_kernel, n_heads=H),
        grid=(B, H),
        in_specs=in_specs,
        out_specs=out_spec,
        out_shape=jax.ShapeDtypeStruct((B, S, H * HEAD_DIM), BF16),
        scratch_shapes=[pltpu.VMEM((gate_rows, HEAD_DIM), F32)],
        compiler_params=_params("parallel", "parallel"),
        name="moba_attn",
    )(qkv, qkv, qkv)


def _outproj_kernel(o_ref, w_ref, x_ref, ga_ref, gn_ref, sc_ref, sh_ref, *rest, n_experts):
    if n_experts:
        wr_hi_ref, wr_lo_ref, x1_ref, h_ref, route_ref = rest
    else:
        x1_ref, h_ref = rest
    x1 = x_ref[...] + ga_ref[0, 0] * _dot(o_ref[...], w_ref[...])
    x1_ref[...] = x1
    h = _norm_mod(x1, gn_ref[...], sc_ref[0, 0], sh_ref[0, 0])
    h_ref[...] = h.astype(h_ref.dtype)
    if n_experts:
        h_hi, h_lo = _split_bf16(h)
        logits = (_dot(h_hi, wr_hi_ref[...]) + _dot(h_hi, wr_lo_ref[...])
                  + _dot(h_lo, wr_hi_ref[...]))
        lane = lax.broadcasted_iota(jnp.int32, logits.shape, 1).astype(F32)
        lg = jnp.where(lane < n_experts, logits, -jnp.inf)
        v1 = jnp.max(lg, axis=-1, keepdims=True)
        i1 = jnp.min(jnp.where(lg == v1, lane, float(LANES)), axis=-1, keepdims=True)
        lg2 = jnp.where(lane == i1, -jnp.inf, lg)
        v2 = jnp.max(lg2, axis=-1, keepdims=True)
        i2 = jnp.min(jnp.where(lg2 == v2, lane, float(LANES)), axis=-1, keepdims=True)
        e2 = jnp.exp(v2 - v1)
        w1 = 1.0 / (1.0 + e2)
        w2 = e2 / (1.0 + e2)
        route_ref[...] = jnp.where(
            lane == 0.0, i1,
            jnp.where(lane == 1.0, i2, jnp.where(lane == 2.0, w1, jnp.where(lane == 3.0, w2, 0.0))))


def _outproj(o2d, w_bf16, x2d, gain, mod_l, seq, h_dtype, router_w=None):
    N, D = x2d.shape
    tm = _tile(seq, 256)
    rows_per_batch = seq // tm
    mod_spec = lambda k: pl.BlockSpec((1, 1, 1, D), lambda i: (i // rows_per_batch, k, 0, 0))
    row_spec = pl.BlockSpec((tm, D), lambda i: (i, 0))
    in_specs = [
        row_spec,
        pl.BlockSpec((D, D), lambda i: (0, 0)),
        row_spec,
        mod_spec(2), pl.BlockSpec((1, D), lambda i: (0, 0)), mod_spec(4), mod_spec(3),
    ]
    args = [o2d, w_bf16, x2d, mod_l, gain.reshape(1, D), mod_l, mod_l]
    out_specs = [row_spec, row_spec]
    out_shape = [jax.ShapeDtypeStruct((N, D), F32), jax.ShapeDtypeStruct((N, D), h_dtype)]
    n_experts = 0
    if router_w is not None:
        n_experts = router_w.shape[1]
        assert n_experts <= LANES
        wr = jnp.zeros((D, LANES), F32).at[:, :n_experts].set(router_w)
        wr_hi = wr.astype(BF16)
        wr_lo = (wr - wr_hi.astype(F32)).astype(BF16)
        in_specs += [pl.BlockSpec((D, LANES), lambda i: (0, 0))] * 2
        args += [wr_hi, wr_lo]
        out_specs.append(pl.BlockSpec((tm, LANES), lambda i: (i, 0)))
        out_shape.append(jax.ShapeDtypeStruct((N, LANES), F32))
    return pl.pallas_call(
        functools.partial(_outproj_kernel, n_experts=n_experts),
        grid=(N // tm,),
        in_specs=in_specs,
        out_specs=out_specs,
        out_shape=out_shape,
        compiler_params=_params("parallel"),
        name="outproj_route" if n_experts else "outproj",
    )(*args)


def _swiglu_act(h, wg, wu):
    g = _dot(h, wg)
    u = _dot(h, wu)
    return (g * (1.0 / (1.0 + jnp.exp(-g))) * u).astype(BF16)


def _ffn_kernel(h_ref, wg_ref, wu_ref, wo_ref, x_ref, gf_ref, out_ref, acc_scr):
    j = pl.program_id(1)

    @pl.when(j == 0)
    def _():
        acc_scr[...] = jnp.zeros_like(acc_scr)

    acc_scr[...] += _dot(_swiglu_act(h_ref[...], wg_ref[...], wu_ref[...]), wo_ref[...])

    @pl.when(j == pl.num_programs(1) - 1)
    def _():
        out_ref[...] = x_ref[...] + gf_ref[0, 0] * acc_scr[...]


def _dense_ffn(h_bf16, w_in_bf16, w_out_bf16, x2d, mod_l, seq):
    N, D = x2d.shape
    d_ff = w_out_bf16.shape[0]
    tm = _tile(seq, 512)
    tn = _tile(d_ff, 512)
    nj = d_ff // tn
    rows_per_batch = seq // tm
    row_spec = pl.BlockSpec((tm, D), lambda i, j: (i, 0))
    return pl.pallas_call(
        _ffn_kernel,
        grid=(N // tm, nj),
        in_specs=[
            row_spec,
            pl.BlockSpec((D, tn), lambda i, j: (0, j)),
            pl.BlockSpec((D, tn), lambda i, j: (0, j + nj)),
            pl.BlockSpec((tn, D), lambda i, j: (j, 0)),
            row_spec,
            pl.BlockSpec((1, 1, 1, D), lambda i, j: (i // rows_per_batch, 5, 0, 0)),
        ],
        out_specs=row_spec,
        out_shape=jax.ShapeDtypeStruct((N, D), F32),
        scratch_shapes=[pltpu.VMEM((tm, D), F32)],
        compiler_params=_params("parallel", "arbitrary"),
        name="dense_ffn",
    )(h_bf16, w_in_bf16, w_in_bf16, w_out_bf16, x2d, mod_l)


DMA_ISSUE_UNROLL = 8


def _moe_kernel(te_ref, nv_ref, rt_ref, h_hbm, wg_ref, wu_ref, wo_ref, y_hbm,
                rows_scr, stage_scr, wg_scr, wu_scr, wo_scr, acc_scr, sem, out_sem, *, sub):
    t = pl.program_id(0)
    j = pl.program_id(1)
    n_valid = nv_ref[t]
    tile_rows = rows_scr.shape[0]
    chunk = stage_scr.shape[1]
    n_chunks = tile_rows // chunk
    valid_chunks = n_valid * (sub // chunk)

    def out_copy(tile):
        return pltpu.make_async_copy(acc_scr, y_hbm.at[pl.ds(tile * tile_rows, tile_rows), :], out_sem)

    def row_copy(s, group, u):
        r = pl.multiple_of(group * DMA_ISSUE_UNROLL, DMA_ISSUE_UNROLL) + u
        tok = rt_ref[t * tile_rows + s * chunk + r]
        return pltpu.make_async_copy(h_hbm.at[pl.ds(tok, 1), :],
                                     stage_scr.at[s % 2, pl.ds(r, 1), :], sem.at[s % 2])

    def issue(s):
        def body(group, c):
            for u in range(DMA_ISSUE_UNROLL):
                row_copy(s, group, u).start()
            return c
        lax.fori_loop(0, chunk // DMA_ISSUE_UNROLL, body, 0)

    def drain(s):
        def body(group, c):
            for u in range(DMA_ISSUE_UNROLL):
                row_copy(s, group, u).wait()
            return c
        lax.fori_loop(0, chunk // DMA_ISSUE_UNROLL, body, 0)
        rows_scr[s * chunk:(s + 1) * chunk, :] = stage_scr[s % 2].astype(BF16)

    @pl.when(j == 0)
    def _():
        pl.when(valid_chunks > 0)(functools.partial(issue, 0))
        for s in range(n_chunks):
            if s + 1 < n_chunks:
                pl.when(s + 1 < valid_chunks)(functools.partial(issue, s + 1))
            pl.when(s < valid_chunks)(functools.partial(drain, s))
        pl.when(t > 0)(lambda: out_copy(t - 1).wait())
        acc_scr[...] = jnp.zeros_like(acc_scr)

    def sub_tile(r0, wg, wu, wo):
        act = _swiglu_act(rows_scr[pl.ds(r0, sub), :], wg, wu)
        acc_scr[pl.ds(r0, sub), :] += _dot(act, wo)

    @pl.when(n_valid > 0)
    def _():
        wg = wg_ref[0].astype(BF16)
        wu = wu_ref[0].astype(BF16)
        wo = wo_ref[0].astype(BF16)
        wg_scr[...] = wg
        wu_scr[...] = wu
        wo_scr[...] = wo
        sub_tile(0, wg, wu, wo)

    def compute(s, c):
        sub_tile(pl.multiple_of(s * sub, sub), wg_scr[...], wu_scr[...], wo_scr[...])
        return c

    lax.fori_loop(1, n_valid, compute, 0)

    @pl.when(j == pl.num_programs(1) - 1)
    def _():
        out_copy(t).start()
        pl.when(t == pl.num_programs(0) - 1)(lambda: out_copy(t).wait())


def _moe_grouped(h_f32, w_in, w_out, tile_expert, tile_nvalid, row_token, tile_rows, sub):
    N, D = h_f32.shape
    E, d_ff, _ = w_out.shape
    n_tiles = tile_expert.shape[0]
    tn = _tile(d_ff, 256)
    nj = d_ff // tn

    def ff_index(t, j, nv):
        return jnp.where(nv[t] > 0, j, nj - 1)

    chunk = sub // 2 if sub % 16 == 0 else sub
    return pl.pallas_call(
        functools.partial(_moe_kernel, sub=sub),
        grid_spec=pltpu.PrefetchScalarGridSpec(
            num_scalar_prefetch=3,
            grid=(n_tiles, nj),
            in_specs=[
                pl.BlockSpec(memory_space=pl.ANY),
                pl.BlockSpec((1, D, tn), lambda t, j, te, nv, rt: (te[t], 0, ff_index(t, j, nv))),
                pl.BlockSpec((1, D, tn), lambda t, j, te, nv, rt: (te[t], 0, ff_index(t, j, nv) + nj)),
                pl.BlockSpec((1, tn, D), lambda t, j, te, nv, rt: (te[t], ff_index(t, j, nv), 0)),
            ],
            out_specs=pl.BlockSpec(memory_space=pl.ANY),
            scratch_shapes=[
                pltpu.VMEM((tile_rows, D), BF16),
                pltpu.VMEM((2, chunk, D), F32),
                pltpu.VMEM((D, tn), BF16),
                pltpu.VMEM((D, tn), BF16),
                pltpu.VMEM((tn, D), BF16),
                pltpu.VMEM((tile_rows, D), F32),
                pltpu.SemaphoreType.DMA((2,)),
                pltpu.SemaphoreType.DMA(()),
            ],
        ),
        out_shape=jax.ShapeDtypeStruct((n_tiles * tile_rows, D), F32),
        compiler_params=_params("arbitrary", "arbitrary"),
        name="moe_grouped",
    )(tile_expert, tile_nvalid, row_token, h_f32, w_in, w_in, w_out)


def _combine_kernel(p0_ref, p1_ref, ys_hbm, x_ref, gf_ref, route_ref, out_ref, buf0, buf1, sem0, sem1):
    i = pl.program_id(0)
    tm = buf0.shape[0]

    def copies(group, u):
        r = pl.multiple_of(group * DMA_ISSUE_UNROLL, DMA_ISSUE_UNROLL) + u
        c0 = pltpu.make_async_copy(ys_hbm.at[pl.ds(p0_ref[i * tm + r], 1), :],
                                   buf0.at[pl.ds(r, 1), :], sem0)
        c1 = pltpu.make_async_copy(ys_hbm.at[pl.ds(p1_ref[i * tm + r], 1), :],
                                   buf1.at[pl.ds(r, 1), :], sem1)
        return c0, c1

    def start(group, c):
        for u in range(DMA_ISSUE_UNROLL):
            c0, c1 = copies(group, u)
            c0.start()
            c1.start()
        return c

    lax.fori_loop(0, tm // DMA_ISSUE_UNROLL, start, 0)

    def wait(group, c):
        for u in range(DMA_ISSUE_UNROLL):
            c0, c1 = copies(group, u)
            c0.wait()
            c1.wait()
        return c

    lax.fori_loop(0, tm // DMA_ISSUE_UNROLL, wait, 0)
    w0 = route_ref[:, MOE_TOPK:MOE_TOPK + 1]
    w1 = route_ref[:, MOE_TOPK + 1:MOE_TOPK + 2]
    out_ref[...] = x_ref[...] + gf_ref[0, 0] * (w0 * buf0[...] + w1 * buf1[...])


def _moe_combine(y_sorted, pos0, pos1, x2d, mod_l, route, seq):
    N, D = x2d.shape
    tm = _tile(seq, 256)
    rows_per_batch = seq // tm
    return pl.pallas_call(
        _combine_kernel,
        grid_spec=pltpu.PrefetchScalarGridSpec(
            num_scalar_prefetch=2,
            grid=(N // tm,),
            in_specs=[
                pl.BlockSpec(memory_space=pl.ANY),
                pl.BlockSpec((tm, D), lambda i, p0, p1: (i, 0)),
                pl.BlockSpec((1, 1, 1, D), lambda i, p0, p1: (i // rows_per_batch, 5, 0, 0)),
                pl.BlockSpec((tm, LANES), lambda i, p0, p1: (i, 0)),
            ],
            out_specs=pl.BlockSpec((tm, D), lambda i, p0, p1: (i, 0)),
            scratch_shapes=[
                pltpu.VMEM((tm, D), F32),
                pltpu.VMEM((tm, D), F32),
                pltpu.SemaphoreType.DMA(()),
                pltpu.SemaphoreType.DMA(()),
            ],
        ),
        out_shape=jax.ShapeDtypeStruct((N, D), F32),
        compiler_params=_params("arbitrary"),
        name="moe_combine",
    )(pos0, pos1, y_sorted, x2d, mod_l, route)


def _moe_tiling(n_assign, n_experts):
    per_expert = n_assign // n_experts
    sub = _tile(per_expert, 512)
    tile_rows = sub * (-(-(per_expert // 2 + sub) // sub))
    n_tiles = n_assign // tile_rows + n_experts
    return sub, tile_rows, n_tiles


def _routing_tables(route, n_experts, sub, tile_rows, n_tiles):
    N = route.shape[0]
    A = N * MOE_TOPK
    expert = route[:, :MOE_TOPK].astype(jnp.int32).reshape(A)
    onehot = (expert[:, None] == jnp.arange(n_experts, dtype=jnp.int32)[None, :]).astype(jnp.int32)
    csum = jnp.cumsum(onehot, axis=0)
    rank = jnp.sum(onehot * csum, axis=1) - 1
    counts = csum[-1]
    tiles_per = (counts + tile_rows - 1) // tile_rows
    tile_end = jnp.cumsum(tiles_per)
    tile_start = tile_end - tiles_per
    dest = tile_start[expert] * tile_rows + rank
    row_token = jnp.zeros((n_tiles * tile_rows,), jnp.int32).at[dest].set(
        jnp.arange(A, dtype=jnp.int32) // MOE_TOPK)
    used = tile_end[-1]
    tix = jnp.arange(n_tiles, dtype=jnp.int32)
    clipped = jnp.minimum(tix, used - 1)
    tile_expert = jnp.sum((clipped[:, None] >= tile_end[None, :]).astype(jnp.int32), axis=1)
    tile_expert = jnp.minimum(tile_expert, n_experts - 1).astype(jnp.int32)
    rows_left = counts[tile_expert] - (tix - tile_start[tile_expert]) * tile_rows
    n_valid = jnp.clip((rows_left + sub - 1) // sub, 0, tile_rows // sub)
    tile_nvalid = jnp.where(tix < used, n_valid, 0).astype(jnp.int32)
    dest2 = dest.reshape(N, MOE_TOPK)
    return tile_expert, tile_nvalid, row_token, dest2[:, 0], dest2[:, 1]


def kernel(x, c, w_ada, b_ada, attn_norm, ffn_norm, w_qkv, w_o, qk_norm_q, qk_norm_k,
           ffn_w_in, ffn_w_out, router_w, moe_w_in, moe_w_out):
    B, S, D = x.shape
    depth = w_ada.shape[0]
    H = D // HEAD_DIM
    N = B * S
    mod = _adaln(c, w_ada, b_ada).reshape(depth, B, 6, 1, D)
    x2d = x.reshape(N, D)
    for i in range(depth):
        mod_l = mod[i]
        wqkv = w_qkv[i].astype(BF16)
        wo = w_o[i].astype(BF16)
        if i % 2 == 0:
            qkv = _qkv_proj(x2d, attn_norm[i], mod_l, wqkv, S, H)
            o = _stickbreak_attention(qkv.reshape(B, S, 3 * D), H)
            x2d, h = _outproj(o.reshape(N, D), wo, x2d, ffn_norm[i], mod_l, S, BF16)
            x2d = _dense_ffn(h, ffn_w_in[i // 2].astype(BF16), ffn_w_out[i // 2].astype(BF16),
                             x2d, mod_l, S)
        else:
            jj = i // 2
            qkv = _qkv_proj(x2d, attn_norm[i], mod_l, wqkv, S, H, (qk_norm_q[jj], qk_norm_k[jj]))
            o = _moba_attention(qkv.reshape(B, S, 3 * D), H)
            x2d, h, route = _outproj(o.reshape(N, D), wo, x2d, ffn_norm[i], mod_l, S, F32,
                                     router_w[jj])
            n_experts = router_w.shape[-1]
            sub, tile_rows, n_tiles = _moe_tiling(N * MOE_TOPK, n_experts)
            te, nv, row_token, pos0, pos1 = _routing_tables(route, n_experts, sub, tile_rows, n_tiles)
            y_sorted = _moe_grouped(h, moe_w_in[jj], moe_w_out[jj], te, nv, row_token, tile_rows, sub)
            x2d = _moe_combine(y_sorted, pos0, pos1, x2d, mod_l, route, S)
    return x2d.reshape(B, S, D)
```

```python
import functools

import jax
import jax.numpy as jnp
from jax import lax
from jax.experimental import pallas as pl
from jax.experimental.pallas import tpu as pltpu

F32 = jnp.float32
BF16 = jnp.bfloat16

HEAD_DIM = 128
MOBA_BLOCK = 256
MOBA_TOPK = 3
MOE_TOPK = 2
EPS = 1e-6
MASKED = -1e30
LANES = 128
BF16_SUBLANES = 16
VMEM_LIMIT = 56 * 1024 * 1024


def _tile(dim, pref):
    t = min(dim, pref)
    while dim % t:
        t //= 2
    return t


def _params(*sem):
    return pltpu.CompilerParams(dimension_semantics=sem, vmem_limit_bytes=VMEM_LIMIT)


def _dot(a, b):
    return jnp.dot(a, b, preferred_element_type=F32)


def _dot_nt(a, b):
    return lax.dot_general(a, b, (((1,), (1,)), ((), ())), preferred_element_type=F32)


def _split_bf16(x):
    hi = x.astype(BF16)
    lo = (x - hi.astype(F32)).astype(BF16)
    return hi, lo


def _adaln_kernel(c_ref, w_ref, b_ref, o_ref):
    acc = _dot(c_ref[...].astype(BF16), w_ref[0].astype(BF16))
    o_ref[0] = acc + b_ref[0]


def _adaln(c, w_ada, b_ada):
    L, D, D6 = w_ada.shape
    B = c.shape[0]
    tn = _tile(D6, 1024)
    return pl.pallas_call(
        _adaln_kernel,
        grid=(L, D6 // tn),
        in_specs=[
            pl.BlockSpec((B, D), lambda l, j: (0, 0)),
            pl.BlockSpec((1, D, tn), lambda l, j: (l, 0, j)),
            pl.BlockSpec((1, 1, tn), lambda l, j: (l, 0, j)),
        ],
        out_specs=pl.BlockSpec((1, B, tn), lambda l, j: (l, 0, j)),
        out_shape=jax.ShapeDtypeStruct((L, B, D6), F32),
        compiler_params=_params("parallel", "parallel"),
        name="adaln",
    )(c, w_ada, b_ada.reshape(L, 1, D6))


def _norm_mod(x, gain, scale, shift):
    ms = jnp.mean(x * x, axis=-1, keepdims=True)
    y = x * lax.rsqrt(ms + EPS) * gain
    return y * (1.0 + scale) + shift


def _qkv_kernel(x_ref, g_ref, sc_ref, sh_ref, w_ref, col_ref, o_ref, h_scr, *, qk_norm, d_model):
    j = pl.program_id(1)
    tn = o_ref.shape[1]

    @pl.when(j == 0)
    def _():
        h = _norm_mod(x_ref[...], g_ref[...], sc_ref[0, 0], sh_ref[0, 0])
        h_scr[...] = h.astype(BF16)

    acc = _dot(h_scr[...], w_ref[...].astype(BF16))
    if not qk_norm:
        o_ref[...] = (acc * col_ref[...]).astype(o_ref.dtype)
    else:
        is_qk = j * tn < 2 * d_model

        @pl.when(is_qk)
        def _():
            for hh in range(tn // HEAD_DIM):
                sl = slice(hh * HEAD_DIM, (hh + 1) * HEAD_DIM)
                blk = acc[:, sl]
                ms = jnp.mean(blk * blk, axis=-1, keepdims=True)
                o_ref[:, sl] = (blk * lax.rsqrt(ms + EPS) * col_ref[:, sl]).astype(o_ref.dtype)

        @pl.when(jnp.logical_not(is_qk))
        def _():
            o_ref[...] = acc.astype(o_ref.dtype)


Q_FOLD = HEAD_DIM ** -0.5 * 1.4426950408889634


def _qkv_proj(x2d, gain, mod_l, w_bf16, seq, n_heads, qk_gains=None):
    N, D = x2d.shape
    D3 = w_bf16.shape[1]
    tm = _tile(seq, 1024)
    tn = _tile(D, 1024)
    assert (2 * D) % tn == 0 and tn % HEAD_DIM == 0
    rows_per_batch = seq // tm
    qk_norm = qk_gains is not None
    gq, gk = qk_gains if qk_norm else (jnp.ones((HEAD_DIM,), F32),) * 2
    col = jnp.concatenate([jnp.tile(gq, n_heads) * Q_FOLD, jnp.tile(gk, n_heads),
                           jnp.ones((D,), F32)]).reshape(1, D3)
    in_specs = [
        pl.BlockSpec((tm, D), lambda i, j: (i, 0)),
        pl.BlockSpec((1, D), lambda i, j: (0, 0)),
        pl.BlockSpec((1, 1, 1, D), lambda i, j: (i // rows_per_batch, 1, 0, 0)),
        pl.BlockSpec((1, 1, 1, D), lambda i, j: (i // rows_per_batch, 0, 0, 0)),
        pl.BlockSpec((D, tn), lambda i, j: (0, j)),
        pl.BlockSpec((1, tn), lambda i, j: (0, j)),
    ]
    args = [x2d, gain.reshape(1, D), mod_l, mod_l, w_bf16, col]
    return pl.pallas_call(
        functools.partial(_qkv_kernel, qk_norm=qk_norm, d_model=D),
        grid=(N // tm, D3 // tn),
        in_specs=in_specs,
        out_specs=pl.BlockSpec((tm, tn), lambda i, j: (i, j)),
        out_shape=jax.ShapeDtypeStruct((N, D3), BF16),
        scratch_shapes=[pltpu.VMEM((tm, D), BF16)],
        compiler_params=_params("parallel", "arbitrary"),
        name="qkv_norm" if qk_norm else "qkv",
    )(*args)


def _dot_tn(a, b):
    return lax.dot_general(a, b, (((0,), (0,)), ((), ())), preferred_element_type=F32)


def _sb_kernel(q_ref, k_ref, v_ref, o_ref, *, tq):
    n_qblocks = q_ref.shape[1] // tq
    key = lax.broadcasted_iota(jnp.int32, (tq, tq), 0)
    qry = lax.broadcasted_iota(jnp.int32, (tq, tq), 1)
    past = key < qry
    later_keys = jnp.where(qry > key, 1.0, 0.0).astype(BF16)

    def rows(ref, i):
        return ref[0, i * tq:(i + 1) * tq, :]

    def score_stage(qi):
        q = rows(q_ref, qi)
        out = []
        for j in range(qi + 1):
            z = _dot_nt(rows(k_ref, j), q)
            log_beta = jnp.minimum(z, 0.0) - jnp.log2(1.0 + jnp.exp2(-jnp.abs(z)))
            log_keep = log_beta - z
            if j == qi:
                log_keep = jnp.where(past, log_keep, 0.0)
            out.append((log_beta, log_keep))
        return out

    def cumsum_stage(parts):
        return [_dot(later_keys, log_keep.astype(BF16)) for (_, log_keep) in parts]

    def value_stage(qi, parts, laters):
        carry = jnp.zeros((1, tq), F32)
        acc = jnp.zeros((HEAD_DIM, tq), F32)
        for j in range(qi, -1, -1):
            log_beta, log_keep = parts[j]
            a = jnp.exp2(log_beta + laters[j] + carry)
            if j == qi:
                a = jnp.where(past, a, 0.0)
            acc = acc + _dot_tn(rows(v_ref, j), a.astype(BF16))
            carry = carry + laters[j][0:1, :] + log_keep[0:1, :]
        o_ref[0, qi * tq:(qi + 1) * tq, :] = acc.T.astype(o_ref.dtype)

    parts = score_stage(0)
    for qi in range(n_qblocks):
        laters = cumsum_stage(parts)
        nxt = score_stage(qi + 1) if qi + 1 < n_qblocks else None
        value_stage(qi, parts, laters)
        parts = nxt


def _head_specs(S, H):
    spec = lambda col: pl.BlockSpec((1, S, HEAD_DIM), lambda b, h: (b, 0, col * H + h))
    return [spec(0), spec(1), spec(2)], spec(0)


def _stickbreak_attention(qkv, n_heads):
    B, S, _ = qkv.shape
    H = n_heads
    in_specs, out_spec = _head_specs(S, H)
    return pl.pallas_call(
        functools.partial(_sb_kernel, tq=_tile(S, 256)),
        grid=(B, H),
        in_specs=in_specs,
        out_specs=out_spec,
        out_shape=jax.ShapeDtypeStruct((B, S, H * HEAD_DIM), BF16),
        compiler_params=_params("parallel", "parallel"),
        name="stickbreak_attn",
    )(qkv, qkv, qkv)


def _moba_kernel(q_ref, k_ref, v_ref, o_ref, kmean_scr, *, n_heads):
    h = pl.program_id(1)
    T = MOBA_BLOCK
    n_blocks = q_ref.shape[1] // T

    def rows(ref, i):
        return ref[0, i * T:(i + 1) * T, :]

    kmean_scr[...] = jnp.zeros_like(kmean_scr)
    for n in range(n_blocks):
        kmean_scr[n:n + 1, :] = jnp.mean(rows(k_ref, n).astype(F32), axis=0, keepdims=True)
    km_hi, km_lo = _split_bf16(kmean_scr[...])
    blk = lax.broadcasted_iota(jnp.int32, (kmean_scr.shape[0], T), 0)
    slope = jnp.exp2(jnp.full((1, 1), h + 1, jnp.int32).astype(F32) * (-8.0 / n_heads))
    slope2 = slope * 1.4426950408889634
    rel = (lax.broadcasted_iota(jnp.int32, (T, T), 1)
           - lax.broadcasted_iota(jnp.int32, (T, T), 0)).astype(F32)
    alibi = [slope2 * (rel + float(d * T)) for d in range(n_blocks)]

    def choose(qi, q):
        gate = _dot_nt(km_hi, q) + _dot_nt(km_lo, q)
        fully_past = blk < qi
        g = jnp.where(fully_past, gate, -jnp.inf)
        beaten = jnp.zeros(gate.shape, F32)
        for m in range(qi):
            gm = g[m:m + 1, :]
            first = jnp.where(blk > m, 1.0, 0.0)
            beaten = beaten + jnp.where(gm > g, 1.0, jnp.where(gm == g, first, 0.0))
        return jnp.where(fully_past, jnp.where(beaten < MOBA_TOPK, 1.0, 0.0), 0.0)

    def score_stage(qi):
        q = rows(q_ref, qi)
        chosen = choose(qi, q)
        scores = []
        m_run = None
        for j in range(qi + 1):
            s = _dot_nt(rows(k_ref, j), q) - alibi[qi - j]
            if j == qi:
                s = jnp.where(rel >= 0.0, s, MASKED)
            else:
                s = jnp.where(chosen[j:j + 1, :] > 0.0, s, MASKED)
            scores.append(s)
            m_j = jnp.max(s, axis=0, keepdims=True)
            m_run = m_j if m_run is None else jnp.maximum(m_run, m_j)
        return scores, m_run

    def value_stage(qi, scores, m_run):
        l_run = jnp.zeros((1, T), F32)
        acc = jnp.zeros((HEAD_DIM, T), F32)
        for j in range(qi + 1):
            p = jnp.exp2(scores[j] - m_run)
            l_run = l_run + jnp.sum(p, axis=0, keepdims=True)
            acc = acc + _dot_tn(rows(v_ref, j), p.astype(BF16))
        o_ref[0, qi * T:(qi + 1) * T, :] = (acc / l_run).T.astype(o_ref.dtype)

    cur = score_stage(0)
    for qi in range(n_blocks):
        nxt = score_stage(qi + 1) if qi + 1 < n_blocks else None
        value_stage(qi, *cur)
        cur = nxt


def _moba_attention(qkv, n_heads):
    B, S, _ = qkv.shape
    H = n_heads
    assert S % MOBA_BLOCK == 0
    nb = S // MOBA_BLOCK
    gate_rows = -(-nb // BF16_SUBLANES) * BF16_SUBLANES
    in_specs, out_spec = _head_specs(S, H)
    return pl.pallas_call(
        functools.partial(_moba_kernel, n_heads=H),
        grid=(B, H),
        in_specs=in_specs,
        out_specs=out_spec,
        out_shape=jax.ShapeDtypeStruct((B, S, H * HEAD_DIM), BF16),
        scratch_shapes=[pltpu.VMEM((gate_rows, HEAD_DIM), F32)],
        compiler_params=_params("parallel", "parallel"),
        name="moba_attn",
    )(qkv, qkv, qkv)


def _outproj_kernel(o_ref, w_ref, x_ref, ga_ref, gn_ref, sc_ref, sh_ref, *rest, n_experts):
    if n_experts:
        wr_hi_ref, wr_lo_ref, x1_ref, h_ref, route_ref = rest
    else:
        x1_ref, h_ref = rest
    x1 = x_ref[...] + ga_ref[0, 0] * _dot(o_ref[...], w_ref[...])
    x1_ref[...] = x1
    h = _norm_mod(x1, gn_ref[...], sc_ref[0, 0], sh_ref[0, 0])
    h_ref[...] = h.astype(h_ref.dtype)
    if n_experts:
        h_hi, h_lo = _split_bf16(h)
        logits = (_dot(h_hi, wr_hi_ref[...]) + _dot(h_hi, wr_lo_ref[...])
                  + _dot(h_lo, wr_hi_ref[...]))
        lane = lax.broadcasted_iota(jnp.int32, logits.shape, 1).astype(F32)
        lg = jnp.where(lane < n_experts, logits, -jnp.inf)
        v1 = jnp.max(lg, axis=-1, keepdims=True)
        i1 = jnp.min(jnp.where(lg == v1, lane, float(LANES)), axis=-1, keepdims=True)
        lg2 = jnp.where(lane == i1, -jnp.inf, lg)
        v2 = jnp.max(lg2, axis=-1, keepdims=True)
        i2 = jnp.min(jnp.where(lg2 == v2, lane, float(LANES)), axis=-1, keepdims=True)
        e2 = jnp.exp(v2 - v1)
        w1 = 1.0 / (1.0 + e2)
        w2 = e2 / (1.0 + e2)
        route_ref[...] = jnp.where(
            lane == 0.0, i1,
            jnp.where(lane == 1.0, i2, jnp.where(lane == 2.0, w1, jnp.where(lane == 3.0, w2, 0.0))))


def _outproj(o2d, w_bf16, x2d, gain, mod_l, seq, h_dtype, router_w=None):
    N, D = x2d.shape
    tm = _tile(seq, 256)
    rows_per_batch = seq // tm
    mod_spec = lambda k: pl.BlockSpec((1, 1, 1, D), lambda i: (i // rows_per_batch, k, 0, 0))
    row_spec = pl.BlockSpec((tm, D), lambda i: (i, 0))
    in_specs = [
        row_spec,
        pl.BlockSpec((D, D), lambda i: (0, 0)),
        row_spec,
        mod_spec(2), pl.BlockSpec((1, D), lambda i: (0, 0)), mod_spec(4), mod_spec(3),
    ]
    args = [o2d, w_bf16, x2d, mod_l, gain.reshape(1, D), mod_l, mod_l]
    out_specs = [row_spec, row_spec]
    out_shape = [jax.ShapeDtypeStruct((N, D), F32), jax.ShapeDtypeStruct((N, D), h_dtype)]
    n_experts = 0
    if router_w is not None:
        n_experts = router_w.shape[1]
        assert n_experts <= LANES
        wr = jnp.zeros((D, LANES), F32).at[:, :n_experts].set(router_w)
        wr_hi = wr.astype(BF16)
        wr_lo = (wr - wr_hi.astype(F32)).astype(BF16)
        in_specs += [pl.BlockSpec((D, LANES), lambda i: (0, 0))] * 2
        args += [wr_hi, wr_lo]
        out_specs.append(pl.BlockSpec((tm, LANES), lambda i: (i, 0)))
        out_shape.append(jax.ShapeDtypeStruct((N, LANES), F32))
    return pl.pallas_call(
        functools.partial(_outproj_kernel, n_experts=n_experts),
        grid=(N // tm,),
        in_specs=in_specs,
        out_specs=out_specs,
        out_shape=out_shape,
        compiler_params=_params("parallel"),
        name="outproj_route" if n_experts else "outproj",
    )(*args)


def _swiglu_act(h, wg, wu):
    g = _dot(h, wg)
    u = _dot(h, wu)
    return (g * (1.0 / (1.0 + jnp.exp(-g))) * u).astype(BF16)


def _ffn_kernel(h_ref, wg_ref, wu_ref, wo_ref, x_ref, gf_ref, out_ref, acc_scr):
    j = pl.program_id(1)

    @pl.when(j == 0)
    def _():
        acc_scr[...] = jnp.zeros_like(acc_scr)

    acc_scr[...] += _dot(_swiglu_act(h_ref[...], wg_ref[...], wu_ref[...]), wo_ref[...])

    @pl.when(j == pl.num_programs(1) - 1)
    def _():
        out_ref[...] = x_ref[...] + gf_ref[0, 0] * acc_scr[...]


def _dense_ffn(h_bf16, w_in_bf16, w_out_bf16, x2d, mod_l, seq):
    N, D = x2d.shape
    d_ff = w_out_bf16.shape[0]
    tm = _tile(seq, 512)
    tn = _tile(d_ff, 512)
    nj = d_ff // tn
    rows_per_batch = seq // tm
    row_spec = pl.BlockSpec((tm, D), lambda i, j: (i, 0))
    return pl.pallas_call(
        _ffn_kernel,
        grid=(N // tm, nj),
        in_specs=[
            row_spec,
            pl.BlockSpec((D, tn), lambda i, j: (0, j)),
            pl.BlockSpec((D, tn), lambda i, j: (0, j + nj)),
            pl.BlockSpec((tn, D), lambda i, j: (j, 0)),
            row_spec,
            pl.BlockSpec((1, 1, 1, D), lambda i, j: (i // rows_per_batch, 5, 0, 0)),
        ],
        out_specs=row_spec,
        out_shape=jax.ShapeDtypeStruct((N, D), F32),
        scratch_shapes=[pltpu.VMEM((tm, D), F32)],
        compiler_params=_params("parallel", "arbitrary"),
        name="dense_ffn",
    )(h_bf16, w_in_bf16, w_in_bf16, w_out_bf16, x2d, mod_l)


DMA_ISSUE_UNROLL = 8


def _moe_kernel(te_ref, nv_ref, rt_ref, h_hbm, wg_ref, wu_ref, wo_ref, y_hbm,
                rows_scr, stage_scr, wg_scr, wu_scr, wo_scr, acc_scr, sem, out_sem, *, sub):
    t = pl.program_id(0)
    j = pl.program_id(1)
    n_valid = nv_ref[t]
    tile_rows = rows_scr.shape[0]
    half = sub // 2
    chunk = stage_scr.shape[1]
    assert chunk == half
    n_chunks = tile_rows // chunk
    valid_chunks = n_valid

    def out_copy(tile):
        return pltpu.make_async_copy(acc_scr, y_hbm.at[pl.ds(tile * tile_rows, tile_rows), :], out_sem)

    def row_copy(s, group, u):
        r = pl.multiple_of(group * DMA_ISSUE_UNROLL, DMA_ISSUE_UNROLL) + u
        tok = rt_ref[t * tile_rows + s * chunk + r]
        return pltpu.make_async_copy(h_hbm.at[pl.ds(tok, 1), :],
                                     stage_scr.at[s % 2, pl.ds(r, 1), :], sem.at[s % 2])

    def issue(s):
        def body(group, c):
            for u in range(DMA_ISSUE_UNROLL):
                row_copy(s, group, u).start()
            return c
        lax.fori_loop(0, chunk // DMA_ISSUE_UNROLL, body, 0)

    def drain(s):
        def body(group, c):
            for u in range(DMA_ISSUE_UNROLL):
                row_copy(s, group, u).wait()
            return c
        lax.fori_loop(0, chunk // DMA_ISSUE_UNROLL, body, 0)
        rows_scr[s * chunk:(s + 1) * chunk, :] = stage_scr[s % 2].astype(BF16)

    @pl.when(j == 0)
    def _():
        pl.when(valid_chunks > 0)(functools.partial(issue, 0))
        for s in range(n_chunks):
            if s + 1 < n_chunks:
                pl.when(s + 1 < valid_chunks)(functools.partial(issue, s + 1))
            pl.when(s < valid_chunks)(functools.partial(drain, s))
        pl.when(t > 0)(lambda: out_copy(t - 1).wait())
        acc_scr[...] = jnp.zeros_like(acc_scr)

    def sub_tile(r0, n_rows, wg, wu, wo):
        act = _swiglu_act(rows_scr[pl.ds(r0, n_rows), :], wg, wu)
        acc_scr[pl.ds(r0, n_rows), :] += _dot(act, wo)

    def from_scratch(r0, n_rows):
        sub_tile(pl.multiple_of(r0, half), n_rows, wg_scr[...], wu_scr[...], wo_scr[...])

    def cast_weights():
        wg = wg_ref[0].astype(BF16)
        wu = wu_ref[0].astype(BF16)
        wo = wo_ref[0].astype(BF16)
        wg_scr[...] = wg
        wu_scr[...] = wu
        wo_scr[...] = wo
        return wg, wu, wo

    n_full = lax.shift_right_logical(n_valid, 1)
    has_half = (n_valid & 1) == 1

    @pl.when(n_full > 0)
    def _():
        sub_tile(0, sub, *cast_weights())

    @pl.when((n_full == 0) & has_half)
    def _():
        cast_weights()

    n_rest = jnp.maximum(n_full - 1, 0)
    n_pairs = lax.shift_right_logical(n_rest, 1)

    def pair(p, c):
        r0 = sub + p * (2 * sub)
        from_scratch(r0, sub)
        from_scratch(r0 + sub, sub)
        return c

    lax.fori_loop(0, n_pairs, pair, 0)
    pl.when((n_rest & 1) == 1)(lambda: from_scratch(sub + n_pairs * (2 * sub), sub))
    pl.when(has_half)(lambda: from_scratch(n_full * sub, half))

    @pl.when(j == pl.num_programs(1) - 1)
    def _():
        out_copy(t).start()
        pl.when(t == pl.num_programs(0) - 1)(lambda: out_copy(t).wait())


def _moe_grouped(h_f32, w_in, w_out, tile_expert, tile_nvalid, row_token, tile_rows, sub):
    N, D = h_f32.shape
    E, d_ff, _ = w_out.shape
    n_tiles = tile_expert.shape[0]
    tn = _tile(d_ff, 256)
    nj = d_ff // tn

    def ff_index(t, j, nv):
        return jnp.where(nv[t] > 0, j, nj - 1)

    assert sub % 16 == 0
    chunk = sub // 2
    return pl.pallas_call(
        functools.partial(_moe_kernel, sub=sub),
        grid_spec=pltpu.PrefetchScalarGridSpec(
            num_scalar_prefetch=3,
            grid=(n_tiles, nj),
            in_specs=[
                pl.BlockSpec(memory_space=pl.ANY),
                pl.BlockSpec((1, D, tn), lambda t, j, te, nv, rt: (te[t], 0, ff_index(t, j, nv))),
                pl.BlockSpec((1, D, tn), lambda t, j, te, nv, rt: (te[t], 0, ff_index(t, j, nv) + nj)),
                pl.BlockSpec((1, tn, D), lambda t, j, te, nv, rt: (te[t], ff_index(t, j, nv), 0)),
            ],
            out_specs=pl.BlockSpec(memory_space=pl.ANY),
            scratch_shapes=[
                pltpu.VMEM((tile_rows, D), BF16),
                pltpu.VMEM((2, chunk, D), F32),
                pltpu.VMEM((D, tn), BF16),
                pltpu.VMEM((D, tn), BF16),
                pltpu.VMEM((tn, D), BF16),
                pltpu.VMEM((tile_rows, D), F32),
                pltpu.SemaphoreType.DMA((2,)),
                pltpu.SemaphoreType.DMA(()),
            ],
        ),
        out_shape=jax.ShapeDtypeStruct((n_tiles * tile_rows, D), F32),
        compiler_params=_params("arbitrary", "arbitrary"),
        name="moe_grouped",
    )(tile_expert, tile_nvalid, row_token, h_f32, w_in, w_in, w_out)


def _combine_kernel(p0_ref, p1_ref, ys_hbm, x_ref, gf_ref, route_ref, out_ref, buf0, buf1, sem0, sem1):
    i = pl.program_id(0)
    tm = buf0.shape[0]

    def copies(group, u):
        r = pl.multiple_of(group * DMA_ISSUE_UNROLL, DMA_ISSUE_UNROLL) + u
        c0 = pltpu.make_async_copy(ys_hbm.at[pl.ds(p0_ref[i * tm + r], 1), :],
                                   buf0.at[pl.ds(r, 1), :], sem0)
        c1 = pltpu.make_async_copy(ys_hbm.at[pl.ds(p1_ref[i * tm + r], 1), :],
                                   buf1.at[pl.ds(r, 1), :], sem1)
        return c0, c1

    def start(group, c):
        for u in range(DMA_ISSUE_UNROLL):
            c0, c1 = copies(group, u)
            c0.start()
            c1.start()
        return c

    lax.fori_loop(0, tm // DMA_ISSUE_UNROLL, start, 0)

    def wait(group, c):
        for u in range(DMA_ISSUE_UNROLL):
            c0, c1 = copies(group, u)
            c0.wait()
            c1.wait()
        return c

    lax.fori_loop(0, tm // DMA_ISSUE_UNROLL, wait, 0)
    w0 = route_ref[:, MOE_TOPK:MOE_TOPK + 1]
    w1 = route_ref[:, MOE_TOPK + 1:MOE_TOPK + 2]
    out_ref[...] = x_ref[...] + gf_ref[0, 0] * (w0 * buf0[...] + w1 * buf1[...])


def _moe_combine(y_sorted, pos0, pos1, x2d, mod_l, route, seq):
    N, D = x2d.shape
    tm = _tile(seq, 256)
    rows_per_batch = seq // tm
    return pl.pallas_call(
        _combine_kernel,
        grid_spec=pltpu.PrefetchScalarGridSpec(
            num_scalar_prefetch=2,
            grid=(N // tm,),
            in_specs=[
                pl.BlockSpec(memory_space=pl.ANY),
                pl.BlockSpec((tm, D), lambda i, p0, p1: (i, 0)),
                pl.BlockSpec((1, 1, 1, D), lambda i, p0, p1: (i // rows_per_batch, 5, 0, 0)),
                pl.BlockSpec((tm, LANES), lambda i, p0, p1: (i, 0)),
            ],
            out_specs=pl.BlockSpec((tm, D), lambda i, p0, p1: (i, 0)),
            scratch_shapes=[
                pltpu.VMEM((tm, D), F32),
                pltpu.VMEM((tm, D), F32),
                pltpu.SemaphoreType.DMA(()),
                pltpu.SemaphoreType.DMA(()),
            ],
        ),
        out_shape=jax.ShapeDtypeStruct((N, D), F32),
        compiler_params=_params("arbitrary"),
        name="moe_combine",
    )(pos0, pos1, y_sorted, x2d, mod_l, route)


def _moe_tiling(n_assign, n_experts):
    per_expert = n_assign // n_experts
    sub = _tile(per_expert, 512)
    tile_rows = sub * (-(-(per_expert // 2 + sub) // sub))
    n_tiles = n_assign // tile_rows + n_experts
    return sub, tile_rows, n_tiles


def _routing_tables(route, n_experts, sub, tile_rows, n_tiles):
    N = route.shape[0]
    A = N * MOE_TOPK
    expert = route[:, :MOE_TOPK].astype(jnp.int32).reshape(A)
    onehot = (expert[:, None] == jnp.arange(n_experts, dtype=jnp.int32)[None, :]).astype(jnp.int32)
    csum = jnp.cumsum(onehot, axis=0)
    rank = jnp.sum(onehot * csum, axis=1) - 1
    counts = csum[-1]
    tiles_per = (counts + tile_rows - 1) // tile_rows
    tile_end = jnp.cumsum(tiles_per)
    tile_start = tile_end - tiles_per
    dest = tile_start[expert] * tile_rows + rank
    row_token = jnp.zeros((n_tiles * tile_rows,), jnp.int32).at[dest].set(
        jnp.arange(A, dtype=jnp.int32) // MOE_TOPK)
    used = tile_end[-1]
    tix = jnp.arange(n_tiles, dtype=jnp.int32)
    clipped = jnp.minimum(tix, used - 1)
    tile_expert = jnp.sum((clipped[:, None] >= tile_end[None, :]).astype(jnp.int32), axis=1)
    tile_expert = jnp.minimum(tile_expert, n_experts - 1).astype(jnp.int32)
    rows_left = counts[tile_expert] - (tix - tile_start[tile_expert]) * tile_rows
    half = sub // 2
    n_valid = jnp.clip((rows_left + half - 1) // half, 0, tile_rows // half)
    tile_nvalid = jnp.where(tix < used, n_valid, 0).astype(jnp.int32)
    dest2 = dest.reshape(N, MOE_TOPK)
    return tile_expert, tile_nvalid, row_token, dest2[:, 0], dest2[:, 1]


def kernel(x, c, w_ada, b_ada, attn_norm, ffn_norm, w_qkv, w_o, qk_norm_q, qk_norm_k,
           ffn_w_in, ffn_w_out, router_w, moe_w_in, moe_w_out):
    B, S, D = x.shape
    depth = w_ada.shape[0]
    H = D // HEAD_DIM
    N = B * S
    mod = _adaln(c, w_ada, b_ada).reshape(depth, B, 6, 1, D)
    x2d = x.reshape(N, D)
    for i in range(depth):
        mod_l = mod[i]
        wqkv = w_qkv[i]
        wo = w_o[i].astype(BF16)
        if i % 2 == 0:
            qkv = _qkv_proj(x2d, attn_norm[i], mod_l, wqkv, S, H)
            o = _stickbreak_attention(qkv.reshape(B, S, 3 * D), H)
            x2d, h = _outproj(o.reshape(N, D), wo, x2d, ffn_norm[i], mod_l, S, BF16)
            x2d = _dense_ffn(h, ffn_w_in[i // 2].astype(BF16), ffn_w_out[i // 2].astype(BF16),
                             x2d, mod_l, S)
        else:
            jj = i // 2
            qkv = _qkv_proj(x2d, attn_norm[i], mod_l, wqkv, S, H, (qk_norm_q[jj], qk_norm_k[jj]))
            o = _moba_attention(qkv.reshape(B, S, 3 * D), H)
            x2d, h, route = _outproj(o.reshape(N, D), wo, x2d, ffn_norm[i], mod_l, S, F32,
                                     router_w[jj])
            n_experts = router_w.shape[-1]
            sub, tile_rows, n_tiles = _moe_tiling(N * MOE_TOPK, n_experts)
            te, nv, row_token, pos0, pos1 = _routing_tables(route, n_experts, sub, tile_rows, n_tiles)
            y_sorted = _moe_grouped(h, moe_w_in[jj], moe_w_out[jj], te, nv, row_token, tile_rows, sub)
            x2d = _moe_combine(y_sorted, pos0, pos1, x2d, mod_l, route, S)
    return x2d.reshape(B, S, D)
```

```python
import functools

import jax
import jax.numpy as jnp
from jax import lax
from jax.experimental import pallas as pl
from jax.experimental.pallas import tpu as pltpu

F32 = jnp.float32
BF16 = jnp.bfloat16

HEAD_DIM = 128
MOBA_BLOCK = 256
MOBA_TOPK = 3
MOE_TOPK = 2
EPS = 1e-6
MASKED = -1e30
LANES = 128
BF16_SUBLANES = 16
VMEM_LIMIT = 56 * 1024 * 1024


def _tile(dim, pref):
    t = min(dim, pref)
    while dim % t:
        t //= 2
    return t


def _params(*sem):
    return pltpu.CompilerParams(dimension_semantics=sem, vmem_limit_bytes=VMEM_LIMIT)


def _dot(a, b):
    return jnp.dot(a, b, preferred_element_type=F32)


def _dot_nt(a, b):
    return lax.dot_general(a, b, (((1,), (1,)), ((), ())), preferred_element_type=F32)


def _split_bf16(x):
    hi = x.astype(BF16)
    lo = (x - hi.astype(F32)).astype(BF16)
    return hi, lo


def _adaln_kernel(c_ref, w_ref, b_ref, o_ref):
    acc = _dot(c_ref[...].astype(BF16), w_ref[0].astype(BF16))
    o_ref[0] = acc + b_ref[0]


def _adaln(c, w_ada, b_ada):
    L, D, D6 = w_ada.shape
    B = c.shape[0]
    tn = _tile(D6, 1024)
    return pl.pallas_call(
        _adaln_kernel,
        grid=(L, D6 // tn),
        in_specs=[
            pl.BlockSpec((B, D), lambda l, j: (0, 0)),
            pl.BlockSpec((1, D, tn), lambda l, j: (l, 0, j)),
            pl.BlockSpec((1, 1, tn), lambda l, j: (l, 0, j)),
        ],
        out_specs=pl.BlockSpec((1, B, tn), lambda l, j: (l, 0, j)),
        out_shape=jax.ShapeDtypeStruct((L, B, D6), F32),
        compiler_params=_params("parallel", "parallel"),
        name="adaln",
    )(c, w_ada, b_ada.reshape(L, 1, D6))


def _norm_mod(x, gain, scale, shift):
    ms = jnp.mean(x * x, axis=-1, keepdims=True)
    y = x * lax.rsqrt(ms + EPS) * gain
    return y * (1.0 + scale) + shift


def _qkv_kernel(x_ref, g_ref, sc_ref, sh_ref, w_ref, col_ref, o_ref, h_scr, *, qk_norm, d_model):
    j = pl.program_id(1)
    tn = o_ref.shape[1]

    @pl.when(j == 0)
    def _():
        h = _norm_mod(x_ref[...], g_ref[...], sc_ref[0, 0], sh_ref[0, 0])
        h_scr[...] = h.astype(BF16)

    acc = _dot(h_scr[...], w_ref[...])
    if not qk_norm:
        o_ref[...] = (acc * col_ref[...]).astype(o_ref.dtype)
    else:
        is_qk = j * tn < 2 * d_model

        @pl.when(is_qk)
        def _():
            for hh in range(tn // HEAD_DIM):
                sl = slice(hh * HEAD_DIM, (hh + 1) * HEAD_DIM)
                blk = acc[:, sl]
                ms = jnp.mean(blk * blk, axis=-1, keepdims=True)
                o_ref[:, sl] = (blk * lax.rsqrt(ms + EPS) * col_ref[:, sl]).astype(o_ref.dtype)

        @pl.when(jnp.logical_not(is_qk))
        def _():
            o_ref[...] = acc.astype(o_ref.dtype)


Q_FOLD = HEAD_DIM ** -0.5 * 1.4426950408889634


def _qkv_proj(x2d, gain, mod_l, w_bf16, seq, n_heads, qk_gains=None):
    N, D = x2d.shape
    D3 = w_bf16.shape[1]
    tm = _tile(seq, 1024)
    tn = _tile(D, 1024)
    assert (2 * D) % tn == 0 and tn % HEAD_DIM == 0
    rows_per_batch = seq // tm
    qk_norm = qk_gains is not None
    gq, gk = qk_gains if qk_norm else (jnp.ones((HEAD_DIM,), F32),) * 2
    col = jnp.concatenate([jnp.tile(gq, n_heads) * Q_FOLD, jnp.tile(gk, n_heads),
                           jnp.ones((D,), F32)]).reshape(1, D3)
    in_specs = [
        pl.BlockSpec((tm, D), lambda i, j: (i, 0)),
        pl.BlockSpec((1, D), lambda i, j: (0, 0)),
        pl.BlockSpec((1, 1, 1, D), lambda i, j: (i // rows_per_batch, 1, 0, 0)),
        pl.BlockSpec((1, 1, 1, D), lambda i, j: (i // rows_per_batch, 0, 0, 0)),
        pl.BlockSpec((D, tn), lambda i, j: (0, j)),
        pl.BlockSpec((1, tn), lambda i, j: (0, j)),
    ]
    args = [x2d, gain.reshape(1, D), mod_l, mod_l, w_bf16, col]
    return pl.pallas_call(
        functools.partial(_qkv_kernel, qk_norm=qk_norm, d_model=D),
        grid=(N // tm, D3 // tn),
        in_specs=in_specs,
        out_specs=pl.BlockSpec((tm, tn), lambda i, j: (i, j)),
        out_shape=jax.ShapeDtypeStruct((N, D3), BF16),
        scratch_shapes=[pltpu.VMEM((tm, D), BF16)],
        compiler_params=_params("parallel", "arbitrary"),
        name="qkv_norm" if qk_norm else "qkv",
    )(*args)


def _dot_tn(a, b):
    return lax.dot_general(a, b, (((0,), (0,)), ((), ())), preferred_element_type=F32)


def _sb_kernel(q_ref, k_ref, v_ref, o_ref, *, tq):
    n_qblocks = q_ref.shape[1] // tq
    key = lax.broadcasted_iota(jnp.int32, (tq, tq), 0)
    qry = lax.broadcasted_iota(jnp.int32, (tq, tq), 1)
    past = key < qry
    later_keys = jnp.where(qry > key, 1.0, 0.0).astype(BF16)

    def rows(ref, i):
        return ref[0, i * tq:(i + 1) * tq, :]

    def score_stage(qi):
        q = rows(q_ref, qi)
        out = []
        for j in range(qi + 1):
            z = _dot_nt(rows(k_ref, j), q)
            log_beta = jnp.minimum(z, 0.0) - jnp.log2(1.0 + jnp.exp2(-jnp.abs(z)))
            log_keep = log_beta - z
            if j == qi:
                log_keep = jnp.where(past, log_keep, 0.0)
            out.append((log_beta, log_keep))
        return out

    def cumsum_stage(parts):
        return [_dot(later_keys, log_keep.astype(BF16)) for (_, log_keep) in parts]

    def value_stage(qi, parts, laters):
        carry = jnp.zeros((1, tq), F32)
        acc = jnp.zeros((HEAD_DIM, tq), F32)
        for j in range(qi, -1, -1):
            log_beta, log_keep = parts[j]
            a = jnp.exp2(log_beta + laters[j] + carry)
            if j == qi:
                a = jnp.where(past, a, 0.0)
            acc = acc + _dot_tn(rows(v_ref, j), a.astype(BF16))
            carry = carry + laters[j][0:1, :] + log_keep[0:1, :]
        o_ref[0, qi * tq:(qi + 1) * tq, :] = acc.T.astype(o_ref.dtype)

    parts = score_stage(0)
    for qi in range(n_qblocks):
        laters = cumsum_stage(parts)
        nxt = score_stage(qi + 1) if qi + 1 < n_qblocks else None
        value_stage(qi, parts, laters)
        parts = nxt


def _head_specs(S, H):
    spec = lambda col: pl.BlockSpec((1, S, HEAD_DIM), lambda b, h: (b, 0, col * H + h))
    return [spec(0), spec(1), spec(2)], spec(0)


def _stickbreak_attention(qkv, n_heads):
    B, S, _ = qkv.shape
    H = n_heads
    in_specs, out_spec = _head_specs(S, H)
    return pl.pallas_call(
        functools.partial(_sb_kernel, tq=_tile(S, 256)),
        grid=(B, H),
        in_specs=in_specs,
        out_specs=out_spec,
        out_shape=jax.ShapeDtypeStruct((B, S, H * HEAD_DIM), BF16),
        compiler_params=_params("parallel", "parallel"),
        name="stickbreak_attn",
    )(qkv, qkv, qkv)


def _moba_kernel(q_ref, k_ref, v_ref, o_ref, kmean_scr, *, n_heads):
    h = pl.program_id(1)
    T = MOBA_BLOCK
    n_blocks = q_ref.shape[1] // T

    def rows(ref, i):
        return ref[0, i * T:(i + 1) * T, :]

    kmean_scr[...] = jnp.zeros_like(kmean_scr)
    for n in range(n_blocks):
        kmean_scr[n:n + 1, :] = jnp.mean(rows(k_ref, n).astype(F32), axis=0, keepdims=True)
    km_hi, km_lo = _split_bf16(kmean_scr[...])
    blk = lax.broadcasted_iota(jnp.int32, (kmean_scr.shape[0], T), 0)
    slope = jnp.exp2(jnp.full((1, 1), h + 1, jnp.int32).astype(F32) * (-8.0 / n_heads))
    slope2 = slope * 1.4426950408889634
    rel = (lax.broadcasted_iota(jnp.int32, (T, T), 1)
           - lax.broadcasted_iota(jnp.int32, (T, T), 0)).astype(F32)
    alibi = [slope2 * (rel + float(d * T)) for d in range(n_blocks)]

    def choose(qi, q):
        gate = _dot_nt(km_hi, q) + _dot_nt(km_lo, q)
        fully_past = blk < qi
        g = jnp.where(fully_past, gate, -jnp.inf)
        beaten = jnp.zeros(gate.shape, F32)
        for m in range(qi):
            gm = g[m:m + 1, :]
            first = jnp.where(blk > m, 1.0, 0.0)
            beaten = beaten + jnp.where(gm > g, 1.0, jnp.where(gm == g, first, 0.0))
        return jnp.where(fully_past, jnp.where(beaten < MOBA_TOPK, 1.0, 0.0), 0.0)

    def score_stage(qi):
        q = rows(q_ref, qi)
        chosen = choose(qi, q)
        scores = []
        m_run = None
        for j in range(qi + 1):
            s = _dot_nt(rows(k_ref, j), q) - alibi[qi - j]
            if j == qi:
                s = jnp.where(rel >= 0.0, s, MASKED)
            else:
                s = jnp.where(chosen[j:j + 1, :] > 0.0, s, MASKED)
            scores.append(s)
            m_j = jnp.max(s, axis=0, keepdims=True)
            m_run = m_j if m_run is None else jnp.maximum(m_run, m_j)
        return scores, m_run

    def value_stage(qi, scores, m_run):
        l_run = jnp.zeros((1, T), F32)
        acc = jnp.zeros((HEAD_DIM, T), F32)
        for j in range(qi + 1):
            p = jnp.exp2(scores[j] - m_run)
            l_run = l_run + jnp.sum(p, axis=0, keepdims=True)
            acc = acc + _dot_tn(rows(v_ref, j), p.astype(BF16))
        o_ref[0, qi * T:(qi + 1) * T, :] = (acc / l_run).T.astype(o_ref.dtype)

    cur = score_stage(0)
    for qi in range(n_blocks):
        nxt = score_stage(qi + 1) if qi + 1 < n_blocks else None
        value_stage(qi, *cur)
        cur = nxt


def _moba_attention(qkv, n_heads):
    B, S, _ = qkv.shape
    H = n_heads
    assert S % MOBA_BLOCK == 0
    nb = S // MOBA_BLOCK
    gate_rows = -(-nb // BF16_SUBLANES) * BF16_SUBLANES
    in_specs, out_spec = _head_specs(S, H)
    return pl.pallas_call(
        functools.partial(_moba_kernel, n_heads=H),
        grid=(B, H),
        in_specs=in_specs,
        out_specs=out_spec,
        out_shape=jax.ShapeDtypeStruct((B, S, H * HEAD_DIM), BF16),
        scratch_shapes=[pltpu.VMEM((gate_rows, HEAD_DIM), F32)],
        compiler_params=_params("parallel", "parallel"),
        name="moba_attn",
    )(qkv, qkv, qkv)


def _outproj_kernel(o_ref, w_ref, x_ref, ga_ref, gn_ref, sc_ref, sh_ref, *rest, n_experts):
    if n_experts:
        wr_ref, x1_ref, h_ref, route_ref = rest
    else:
        x1_ref, h_ref = rest
    x1 = x_ref[...] + ga_ref[0, 0] * _dot(o_ref[...], w_ref[...])
    x1_ref[...] = x1
    h = _norm_mod(x1, gn_ref[...], sc_ref[0, 0], sh_ref[0, 0])
    h_ref[...] = h.astype(h_ref.dtype)
    if n_experts:
        h_hi, h_lo = _split_bf16(h)
        both = _dot(h_hi, wr_ref[...])
        logits = both[:, :LANES] + both[:, LANES:] + _dot(h_lo, wr_ref[:, :LANES])
        lane = lax.broadcasted_iota(jnp.int32, logits.shape, 1).astype(F32)
        lg = jnp.where(lane < n_experts, logits, -jnp.inf)
        v1 = jnp.max(lg, axis=-1, keepdims=True)
        i1 = jnp.min(jnp.where(lg == v1, lane, float(LANES)), axis=-1, keepdims=True)
        lg2 = jnp.where(lane == i1, -jnp.inf, lg)
        v2 = jnp.max(lg2, axis=-1, keepdims=True)
        i2 = jnp.min(jnp.where(lg2 == v2, lane, float(LANES)), axis=-1, keepdims=True)
        e2 = jnp.exp(v2 - v1)
        w1 = 1.0 / (1.0 + e2)
        w2 = e2 / (1.0 + e2)
        route_ref[...] = jnp.where(
            lane == 0.0, i1,
            jnp.where(lane == 1.0, i2, jnp.where(lane == 2.0, w1, jnp.where(lane == 3.0, w2, 0.0))))


def _outproj(o2d, w_bf16, x2d, gain, mod_l, seq, h_dtype, router_w=None):
    N, D = x2d.shape
    tm = _tile(seq, 256)
    rows_per_batch = seq // tm
    mod_spec = lambda k: pl.BlockSpec((1, 1, 1, D), lambda i: (i // rows_per_batch, k, 0, 0))
    row_spec = pl.BlockSpec((tm, D), lambda i: (i, 0))
    in_specs = [
        row_spec,
        pl.BlockSpec((D, D), lambda i: (0, 0)),
        row_spec,
        mod_spec(2), pl.BlockSpec((1, D), lambda i: (0, 0)), mod_spec(4), mod_spec(3),
    ]
    args = [o2d, w_bf16, x2d, mod_l, gain.reshape(1, D), mod_l, mod_l]
    out_specs = [row_spec, row_spec]
    out_shape = [jax.ShapeDtypeStruct((N, D), F32), jax.ShapeDtypeStruct((N, D), h_dtype)]
    n_experts = 0
    if router_w is not None:
        n_experts = router_w.shape[1]
        assert n_experts <= LANES
        wr = jnp.zeros((D, LANES), F32).at[:, :n_experts].set(router_w)
        wr_hi = wr.astype(BF16)
        wr_lo = (wr - wr_hi.astype(F32)).astype(BF16)
        in_specs.append(pl.BlockSpec((D, 2 * LANES), lambda i: (0, 0)))
        args.append(jnp.concatenate([wr_hi, wr_lo], axis=1))
        out_specs.append(pl.BlockSpec((tm, LANES), lambda i: (i, 0)))
        out_shape.append(jax.ShapeDtypeStruct((N, LANES), F32))
    return pl.pallas_call(
        functools.partial(_outproj_kernel, n_experts=n_experts),
        grid=(N // tm,),
        in_specs=in_specs,
        out_specs=out_specs,
        out_shape=out_shape,
        compiler_params=_params("parallel"),
        name="outproj_route" if n_experts else "outproj",
    )(*args)


def _swiglu_act(h, wg, wu):
    g = _dot(h, wg)
    u = _dot(h, wu)
    return (g * (1.0 / (1.0 + jnp.exp(-g))) * u).astype(BF16)


def _ffn_kernel(h_ref, wg_ref, wu_ref, wo_ref, x_ref, gf_ref, out_ref, acc_scr):
    j = pl.program_id(1)

    @pl.when(j == 0)
    def _():
        acc_scr[...] = jnp.zeros_like(acc_scr)

    acc_scr[...] += _dot(_swiglu_act(h_ref[...], wg_ref[...], wu_ref[...]), wo_ref[...])

    @pl.when(j == pl.num_programs(1) - 1)
    def _():
        out_ref[...] = x_ref[...] + gf_ref[0, 0] * acc_scr[...]


def _dense_ffn(h_bf16, w_in_bf16, w_out_bf16, x2d, mod_l, seq):
    N, D = x2d.shape
    d_ff = w_out_bf16.shape[0]
    tm = _tile(seq, 512)
    tn = _tile(d_ff, 512)
    nj = d_ff // tn
    rows_per_batch = seq // tm
    row_spec = pl.BlockSpec((tm, D), lambda i, j: (i, 0))
    return pl.pallas_call(
        _ffn_kernel,
        grid=(N // tm, nj),
        in_specs=[
            row_spec,
            pl.BlockSpec((D, tn), lambda i, j: (0, j)),
            pl.BlockSpec((D, tn), lambda i, j: (0, j + nj)),
            pl.BlockSpec((tn, D), lambda i, j: (j, 0)),
            row_spec,
            pl.BlockSpec((1, 1, 1, D), lambda i, j: (i // rows_per_batch, 5, 0, 0)),
        ],
        out_specs=row_spec,
        out_shape=jax.ShapeDtypeStruct((N, D), F32),
        scratch_shapes=[pltpu.VMEM((tm, D), F32)],
        compiler_params=_params("parallel", "arbitrary"),
        name="dense_ffn",
    )(h_bf16, w_in_bf16, w_in_bf16, w_out_bf16, x2d, mod_l)


ROW_GROUP = 8


def _moe_kernel(te_ref, nv_ref, rt_ref, h_hbm, wg_ref, wu_ref, wo_ref, y_hbm,
                rows_scr, stage_scr, wg_scr, wu_scr, wo_scr, acc_scr, sem, out_sem, *, sub):
    t = pl.program_id(0)
    j = pl.program_id(1)
    n_valid = nv_ref[t]
    tile_rows = rows_scr.shape[0]
    half = sub // 2
    chunk = stage_scr.shape[1] * ROW_GROUP
    assert chunk == half
    n_chunks = tile_rows // chunk
    valid_chunks = n_valid

    def out_copy(tile):
        return pltpu.make_async_copy(acc_scr, y_hbm.at[pl.ds(tile * tile_rows, tile_rows), :], out_sem)

    def row_copy(s, group, u):
        tok = rt_ref[t * tile_rows + s * chunk + group * ROW_GROUP + u]
        return pltpu.make_async_copy(h_hbm.at[pl.ds(tok, 1), :],
                                     stage_scr.at[s % 2, group, pl.ds(u, 1), :], sem.at[s % 2])

    def issue(s):
        def body(group, c):
            for u in range(ROW_GROUP):
                row_copy(s, group, u).start()
            return c
        lax.fori_loop(0, chunk // ROW_GROUP, body, 0)

    def drain(s):
        def body(group, c):
            for u in range(ROW_GROUP):
                row_copy(s, group, u).wait()
            return c
        lax.fori_loop(0, chunk // ROW_GROUP, body, 0)
        staged = stage_scr[s % 2].reshape(chunk, stage_scr.shape[3])
        rows_scr[s * chunk:(s + 1) * chunk, :] = staged.astype(BF16)

    @pl.when(j == 0)
    def _():
        pl.when(valid_chunks > 0)(functools.partial(issue, 0))
        for s in range(n_chunks):
            if s + 1 < n_chunks:
                pl.when(s + 1 < valid_chunks)(functools.partial(issue, s + 1))
            pl.when(s < valid_chunks)(functools.partial(drain, s))
        pl.when(t > 0)(lambda: out_copy(t - 1).wait())
        acc_scr[...] = jnp.zeros_like(acc_scr)

    def sub_tile(r0, n_rows, wg, wu, wo):
        act = _swiglu_act(rows_scr[pl.ds(r0, n_rows), :], wg, wu)
        acc_scr[pl.ds(r0, n_rows), :] += _dot(act, wo)

    def from_scratch(r0, n_rows):
        sub_tile(pl.multiple_of(r0, half), n_rows, wg_scr[...], wu_scr[...], wo_scr[...])

    def cast_weights():
        wg = wg_ref[0].astype(BF16)
        wu = wu_ref[0].astype(BF16)
        wo = wo_ref[0].astype(BF16)
        wg_scr[...] = wg
        wu_scr[...] = wu
        wo_scr[...] = wo
        return wg, wu, wo

    n_full = lax.shift_right_logical(n_valid, 1)
    has_half = (n_valid & 1) == 1

    @pl.when(n_full > 0)
    def _():
        sub_tile(0, sub, *cast_weights())

    @pl.when((n_full == 0) & has_half)
    def _():
        cast_weights()

    n_rest = jnp.maximum(n_full - 1, 0)
    n_pairs = lax.shift_right_logical(n_rest, 1)

    def pair(p, c):
        r0 = sub + p * (2 * sub)
        from_scratch(r0, sub)
        from_scratch(r0 + sub, sub)
        return c

    lax.fori_loop(0, n_pairs, pair, 0)
    pl.when((n_rest & 1) == 1)(lambda: from_scratch(sub + n_pairs * (2 * sub), sub))
    pl.when(has_half)(lambda: from_scratch(n_full * sub, half))

    @pl.when(j == pl.num_programs(1) - 1)
    def _():
        out_copy(t).start()
        pl.when(t == pl.num_programs(0) - 1)(lambda: out_copy(t).wait())


def _moe_grouped(h_f32, w_in, w_out, tile_expert, tile_nvalid, row_token, tile_rows, sub):
    N, D = h_f32.shape
    E, d_ff, _ = w_out.shape
    n_tiles = tile_expert.shape[0]
    tn = _tile(d_ff, 256)
    nj = d_ff // tn

    def ff_index(t, j, nv):
        return jnp.where(nv[t] > 0, j, nj - 1)

    assert sub % 16 == 0
    chunk = sub // 2
    return pl.pallas_call(
        functools.partial(_moe_kernel, sub=sub),
        grid_spec=pltpu.PrefetchScalarGridSpec(
            num_scalar_prefetch=3,
            grid=(n_tiles, nj),
            in_specs=[
                pl.BlockSpec(memory_space=pl.ANY),
                pl.BlockSpec((1, D, tn), lambda t, j, te, nv, rt: (te[t], 0, ff_index(t, j, nv))),
                pl.BlockSpec((1, D, tn), lambda t, j, te, nv, rt: (te[t], 0, ff_index(t, j, nv) + nj)),
                pl.BlockSpec((1, tn, D), lambda t, j, te, nv, rt: (te[t], ff_index(t, j, nv), 0)),
            ],
            out_specs=pl.BlockSpec(memory_space=pl.ANY),
            scratch_shapes=[
                pltpu.VMEM((tile_rows, D), BF16),
                pltpu.VMEM((2, chunk // ROW_GROUP, ROW_GROUP, D), F32),
                pltpu.VMEM((D, tn), BF16),
                pltpu.VMEM((D, tn), BF16),
                pltpu.VMEM((tn, D), BF16),
                pltpu.VMEM((tile_rows, D), F32),
                pltpu.SemaphoreType.DMA((2,)),
                pltpu.SemaphoreType.DMA(()),
            ],
        ),
        out_shape=jax.ShapeDtypeStruct((n_tiles * tile_rows, D), F32),
        compiler_params=_params("arbitrary", "arbitrary"),
        name="moe_grouped",
    )(tile_expert, tile_nvalid, row_token, h_f32, w_in, w_in, w_out)


def _combine_kernel(p0_ref, p1_ref, ys_hbm, x_ref, gf_ref, route_ref, out_ref, buf0, buf1, sem0, sem1):
    i = pl.program_id(0)
    n_groups, _, D = buf0.shape
    tm = n_groups * ROW_GROUP

    def copies(group, u):
        r = i * tm + group * ROW_GROUP + u
        c0 = pltpu.make_async_copy(ys_hbm.at[pl.ds(p0_ref[r], 1), :], buf0.at[group, pl.ds(u, 1), :], sem0)
        c1 = pltpu.make_async_copy(ys_hbm.at[pl.ds(p1_ref[r], 1), :], buf1.at[group, pl.ds(u, 1), :], sem1)
        return c0, c1

    def start(group, c):
        for u in range(ROW_GROUP):
            c0, c1 = copies(group, u)
            c0.start()
            c1.start()
        return c

    lax.fori_loop(0, n_groups, start, 0)

    def wait(group, c):
        for u in range(ROW_GROUP):
            c0, c1 = copies(group, u)
            c0.wait()
            c1.wait()
        return c

    lax.fori_loop(0, n_groups, wait, 0)
    w0 = route_ref[:, MOE_TOPK:MOE_TOPK + 1]
    w1 = route_ref[:, MOE_TOPK + 1:MOE_TOPK + 2]
    y0 = buf0[...].reshape(tm, D)
    y1 = buf1[...].reshape(tm, D)
    out_ref[...] = x_ref[...] + gf_ref[0, 0] * (w0 * y0 + w1 * y1)


def _moe_combine(y_sorted, pos0, pos1, x2d, mod_l, route, seq):
    N, D = x2d.shape
    tm = _tile(seq, 256)
    rows_per_batch = seq // tm
    return pl.pallas_call(
        _combine_kernel,
        grid_spec=pltpu.PrefetchScalarGridSpec(
            num_scalar_prefetch=2,
            grid=(N // tm,),
            in_specs=[
                pl.BlockSpec(memory_space=pl.ANY),
                pl.BlockSpec((tm, D), lambda i, p0, p1: (i, 0)),
                pl.BlockSpec((1, 1, 1, D), lambda i, p0, p1: (i // rows_per_batch, 5, 0, 0)),
                pl.BlockSpec((tm, LANES), lambda i, p0, p1: (i, 0)),
            ],
            out_specs=pl.BlockSpec((tm, D), lambda i, p0, p1: (i, 0)),
            scratch_shapes=[
                pltpu.VMEM((tm // ROW_GROUP, ROW_GROUP, D), F32),
                pltpu.VMEM((tm // ROW_GROUP, ROW_GROUP, D), F32),
                pltpu.SemaphoreType.DMA(()),
                pltpu.SemaphoreType.DMA(()),
            ],
        ),
        out_shape=jax.ShapeDtypeStruct((N, D), F32),
        compiler_params=_params("arbitrary"),
        name="moe_combine",
    )(pos0, pos1, y_sorted, x2d, mod_l, route)


def _moe_tiling(n_assign, n_experts):
    per_expert = n_assign // n_experts
    sub = _tile(per_expert, 512)
    tile_rows = sub * (-(-(per_expert // 2 + sub) // sub))
    n_tiles = n_assign // tile_rows + n_experts
    return sub, tile_rows, n_tiles


def _routing_tables(route, n_experts, sub, tile_rows, n_tiles):
    N = route.shape[0]
    A = N * MOE_TOPK
    expert = route[:, :MOE_TOPK].astype(jnp.int32).reshape(A)
    onehot = (expert[:, None] == jnp.arange(n_experts, dtype=jnp.int32)[None, :]).astype(jnp.int32)
    csum = jnp.cumsum(onehot, axis=0)
    rank = jnp.sum(onehot * csum, axis=1) - 1
    counts = csum[-1]
    tiles_per = (counts + tile_rows - 1) // tile_rows
    tile_end = jnp.cumsum(tiles_per)
    tile_start = tile_end - tiles_per
    dest = tile_start[expert] * tile_rows + rank
    row_token = jnp.zeros((n_tiles * tile_rows,), jnp.int32).at[dest].set(
        jnp.arange(A, dtype=jnp.int32) // MOE_TOPK)
    used = tile_end[-1]
    tix = jnp.arange(n_tiles, dtype=jnp.int32)
    clipped = jnp.minimum(tix, used - 1)
    tile_expert = jnp.sum((clipped[:, None] >= tile_end[None, :]).astype(jnp.int32), axis=1)
    tile_expert = jnp.minimum(tile_expert, n_experts - 1).astype(jnp.int32)
    rows_left = counts[tile_expert] - (tix - tile_start[tile_expert]) * tile_rows
    half = sub // 2
    n_valid = jnp.clip((rows_left + half - 1) // half, 0, tile_rows // half)
    tile_nvalid = jnp.where(tix < used, n_valid, 0).astype(jnp.int32)
    dest2 = dest.reshape(N, MOE_TOPK)
    return tile_expert, tile_nvalid, row_token, dest2[:, 0], dest2[:, 1]


def kernel(x, c, w_ada, b_ada, attn_norm, ffn_norm, w_qkv, w_o, qk_norm_q, qk_norm_k,
           ffn_w_in, ffn_w_out, router_w, moe_w_in, moe_w_out):
    B, S, D = x.shape
    depth = w_ada.shape[0]
    H = D // HEAD_DIM
    N = B * S
    mod = _adaln(c, w_ada, b_ada).reshape(depth, B, 6, 1, D)
    x2d = x.reshape(N, D)
    for i in range(depth):
        mod_l = mod[i]
        wqkv = w_qkv[i].astype(BF16)
        wo = w_o[i].astype(BF16)
        if i % 2 == 0:
            qkv = _qkv_proj(x2d, attn_norm[i], mod_l, wqkv, S, H)
            o = _stickbreak_attention(qkv.reshape(B, S, 3 * D), H)
            x2d, h = _outproj(o.reshape(N, D), wo, x2d, ffn_norm[i], mod_l, S, BF16)
            x2d = _dense_ffn(h, ffn_w_in[i // 2].astype(BF16), ffn_w_out[i // 2].astype(BF16),
                             x2d, mod_l, S)
        else:
            jj = i // 2
            qkv = _qkv_proj(x2d, attn_norm[i], mod_l, wqkv, S, H, (qk_norm_q[jj], qk_norm_k[jj]))
            o = _moba_attention(qkv.reshape(B, S, 3 * D), H)
            x2d, h, route = _outproj(o.reshape(N, D), wo, x2d, ffn_norm[i], mod_l, S, F32,
                                     router_w[jj])
            n_experts = router_w.shape[-1]
            sub, tile_rows, n_tiles = _moe_tiling(N * MOE_TOPK, n_experts)
            te, nv, row_token, pos0, pos1 = _routing_tables(route, n_experts, sub, tile_rows, n_tiles)
            y_sorted = _moe_grouped(h, moe_w_in[jj], moe_w_out[jj], te, nv, row_token, tile_rows, sub)
            x2d = _moe_combine(y_sorted, pos0, pos1, x2d, mod_l, route, S)
    return x2d.reshape(B, S, D)
```

```python
import functools

import jax
import jax.numpy as jnp
from jax import lax
from jax.experimental import pallas as pl
from jax.experimental.pallas import tpu as pltpu

F32 = jnp.float32
BF16 = jnp.bfloat16

HEAD_DIM = 128
MOBA_BLOCK = 256
MOBA_TOPK = 3
MOE_TOPK = 2
EPS = 1e-6
MASKED = -1e30
LANES = 128
BF16_SUBLANES = 16
VMEM_LIMIT = 56 * 1024 * 1024


def _tile(dim, pref):
    t = min(dim, pref)
    while dim % t:
        t //= 2
    return t


def _params(*sem):
    return pltpu.CompilerParams(dimension_semantics=sem, vmem_limit_bytes=VMEM_LIMIT)


def _dot(a, b):
    return jnp.dot(a, b, preferred_element_type=F32)


def _dot_nt(a, b):
    return lax.dot_general(a, b, (((1,), (1,)), ((), ())), preferred_element_type=F32)


def _split_bf16(x):
    hi = x.astype(BF16)
    lo = (x - hi.astype(F32)).astype(BF16)
    return hi, lo


def _adaln_kernel(c_ref, w_ref, b_ref, o_ref):
    acc = _dot(c_ref[...].astype(BF16), w_ref[0].astype(BF16))
    o_ref[0] = acc + b_ref[0]


def _adaln(c, w_ada, b_ada):
    L, D, D6 = w_ada.shape
    B = c.shape[0]
    tn = _tile(D6, 1024)
    return pl.pallas_call(
        _adaln_kernel,
        grid=(L, D6 // tn),
        in_specs=[
            pl.BlockSpec((B, D), lambda l, j: (0, 0)),
            pl.BlockSpec((1, D, tn), lambda l, j: (l, 0, j)),
            pl.BlockSpec((1, 1, tn), lambda l, j: (l, 0, j)),
        ],
        out_specs=pl.BlockSpec((1, B, tn), lambda l, j: (l, 0, j)),
        out_shape=jax.ShapeDtypeStruct((L, B, D6), F32),
        compiler_params=_params("parallel", "parallel"),
        name="adaln",
    )(c, w_ada, b_ada.reshape(L, 1, D6))


def _norm_mod(x, gain, scale, shift):
    ms = jnp.mean(x * x, axis=-1, keepdims=True)
    y = x * lax.rsqrt(ms + EPS) * gain
    return y * (1.0 + scale) + shift


def _qkv_kernel(x_ref, g_ref, sc_ref, sh_ref, w_ref, col_ref, o_ref, h_scr, *, qk_norm, d_model):
    j = pl.program_id(1)
    tn = o_ref.shape[1]

    @pl.when(j == 0)
    def _():
        h = _norm_mod(x_ref[...], g_ref[...], sc_ref[0, 0], sh_ref[0, 0])
        h_scr[...] = h.astype(BF16)

    acc = _dot(h_scr[...], w_ref[...])
    if not qk_norm:
        o_ref[...] = (acc * col_ref[...]).astype(o_ref.dtype)
    else:
        is_qk = j * tn < 2 * d_model

        @pl.when(is_qk)
        def _():
            for hh in range(tn // HEAD_DIM):
                sl = slice(hh * HEAD_DIM, (hh + 1) * HEAD_DIM)
                blk = acc[:, sl]
                ms = jnp.mean(blk * blk, axis=-1, keepdims=True)
                o_ref[:, sl] = (blk * lax.rsqrt(ms + EPS) * col_ref[:, sl]).astype(o_ref.dtype)

        @pl.when(jnp.logical_not(is_qk))
        def _():
            o_ref[...] = acc.astype(o_ref.dtype)


Q_FOLD = HEAD_DIM ** -0.5 * 1.4426950408889634


def _qkv_proj(x2d, gain, mod_l, w_bf16, seq, n_heads, qk_gains=None):
    N, D = x2d.shape
    D3 = w_bf16.shape[1]
    tm = _tile(seq, 1024)
    tn = _tile(D, 1024)
    assert (2 * D) % tn == 0 and tn % HEAD_DIM == 0
    rows_per_batch = seq // tm
    qk_norm = qk_gains is not None
    gq, gk = qk_gains if qk_norm else (jnp.ones((HEAD_DIM,), F32),) * 2
    col = jnp.concatenate([jnp.tile(gq, n_heads) * Q_FOLD, jnp.tile(gk, n_heads),
                           jnp.ones((D,), F32)]).reshape(1, D3)
    in_specs = [
        pl.BlockSpec((tm, D), lambda i, j: (i, 0)),
        pl.BlockSpec((1, D), lambda i, j: (0, 0)),
        pl.BlockSpec((1, 1, 1, D), lambda i, j: (i // rows_per_batch, 1, 0, 0)),
        pl.BlockSpec((1, 1, 1, D), lambda i, j: (i // rows_per_batch, 0, 0, 0)),
        pl.BlockSpec((D, tn), lambda i, j: (0, j)),
        pl.BlockSpec((1, tn), lambda i, j: (0, j)),
    ]
    args = [x2d, gain.reshape(1, D), mod_l, mod_l, w_bf16, col]
    return pl.pallas_call(
        functools.partial(_qkv_kernel, qk_norm=qk_norm, d_model=D),
        grid=(N // tm, D3 // tn),
        in_specs=in_specs,
        out_specs=pl.BlockSpec((tm, tn), lambda i, j: (i, j)),
        out_shape=jax.ShapeDtypeStruct((N, D3), BF16),
        scratch_shapes=[pltpu.VMEM((tm, D), BF16)],
        compiler_params=_params("parallel", "arbitrary"),
        name="qkv_norm" if qk_norm else "qkv",
    )(*args)


def _dot_tn(a, b):
    return lax.dot_general(a, b, (((0,), (0,)), ((), ())), preferred_element_type=F32)


def _sb_kernel(q_ref, k_ref, v_ref, o_ref, *, tq):
    n_qblocks = q_ref.shape[1] // tq
    key = lax.broadcasted_iota(jnp.int32, (tq, tq), 0)
    qry = lax.broadcasted_iota(jnp.int32, (tq, tq), 1)
    past = key < qry
    later_keys = jnp.where(qry > key, 1.0, 0.0).astype(BF16)

    def rows(ref, i):
        return ref[0, i * tq:(i + 1) * tq, :]

    def score_stage(qi):
        q = rows(q_ref, qi)
        out = []
        for j in range(qi + 1):
            z = _dot_nt(rows(k_ref, j), q)
            log_beta = jnp.minimum(z, 0.0) - jnp.log2(1.0 + jnp.exp2(-jnp.abs(z)))
            log_keep = log_beta - z
            if j == qi:
                log_keep = jnp.where(past, log_keep, 0.0)
            out.append((log_beta, log_keep))
        return out

    def cumsum_stage(parts):
        return [_dot(later_keys, log_keep.astype(BF16)) for (_, log_keep) in parts]

    def value_stage(qi, parts, laters):
        carry = jnp.zeros((1, tq), F32)
        acc = jnp.zeros((HEAD_DIM, tq), F32)
        for j in range(qi, -1, -1):
            log_beta, log_keep = parts[j]
            a = jnp.exp2(log_beta + laters[j] + carry)
            if j == qi:
                a = jnp.where(past, a, 0.0)
            acc = acc + _dot_tn(rows(v_ref, j), a.astype(BF16))
            carry = carry + laters[j][0:1, :] + log_keep[0:1, :]
        o_ref[0, qi * tq:(qi + 1) * tq, :] = acc.T.astype(o_ref.dtype)

    parts = score_stage(0)
    for qi in range(n_qblocks):
        laters = cumsum_stage(parts)
        nxt = score_stage(qi + 1) if qi + 1 < n_qblocks else None
        value_stage(qi, parts, laters)
        parts = nxt


def _head_specs(S, H):
    spec = lambda col: pl.BlockSpec((1, S, HEAD_DIM), lambda b, h: (b, 0, col * H + h))
    return [spec(0), spec(1), spec(2)], spec(0)


def _stickbreak_attention(qkv, n_heads):
    B, S, _ = qkv.shape
    H = n_heads
    in_specs, out_spec = _head_specs(S, H)
    return pl.pallas_call(
        functools.partial(_sb_kernel, tq=_tile(S, 256)),
        grid=(B, H),
        in_specs=in_specs,
        out_specs=out_spec,
        out_shape=jax.ShapeDtypeStruct((B, S, H * HEAD_DIM), BF16),
        compiler_params=_params("parallel", "parallel"),
        name="stickbreak_attn",
    )(qkv, qkv, qkv)


def _moba_kernel(q_ref, k_ref, v_ref, o_ref, kmean_scr, *, n_heads):
    h = pl.program_id(1)
    T = MOBA_BLOCK
    n_blocks = q_ref.shape[1] // T

    def rows(ref, i):
        return ref[0, i * T:(i + 1) * T, :]

    kmean_scr[...] = jnp.zeros_like(kmean_scr)
    for n in range(n_blocks):
        kmean_scr[n:n + 1, :] = jnp.mean(rows(k_ref, n).astype(F32), axis=0, keepdims=True)
    km_hi, km_lo = _split_bf16(kmean_scr[...])
    blk = lax.broadcasted_iota(jnp.int32, (kmean_scr.shape[0], T), 0)
    slope = jnp.exp2(jnp.full((1, 1), h + 1, jnp.int32).astype(F32) * (-8.0 / n_heads))
    slope2 = slope * 1.4426950408889634
    rel = (lax.broadcasted_iota(jnp.int32, (T, T), 1)
           - lax.broadcasted_iota(jnp.int32, (T, T), 0)).astype(F32)
    alibi = [slope2 * (rel + float(d * T)) for d in range(n_blocks)]

    def choose(qi, q):
        gate = _dot_nt(km_hi, q) + _dot_nt(km_lo, q)
        fully_past = blk < qi
        g = jnp.where(fully_past, gate, -jnp.inf)
        beaten = jnp.zeros(gate.shape, F32)
        for m in range(qi):
            gm = g[m:m + 1, :]
            first = jnp.where(blk > m, 1.0, 0.0)
            beaten = beaten + jnp.where(gm > g, 1.0, jnp.where(gm == g, first, 0.0))
        return jnp.where(fully_past, jnp.where(beaten < MOBA_TOPK, 1.0, 0.0), 0.0)

    def score_stage(qi):
        q = rows(q_ref, qi)
        chosen = choose(qi, q)
        scores = []
        m_run = None
        for j in range(qi + 1):
            s = _dot_nt(rows(k_ref, j), q) - alibi[qi - j]
            if j == qi:
                s = jnp.where(rel >= 0.0, s, MASKED)
            else:
                s = jnp.where(chosen[j:j + 1, :] > 0.0, s, MASKED)
            scores.append(s)
            m_j = jnp.max(s, axis=0, keepdims=True)
            m_run = m_j if m_run is None else jnp.maximum(m_run, m_j)
        return scores, m_run

    def value_stage(qi, scores, m_run):
        l_run = jnp.zeros((1, T), F32)
        acc = jnp.zeros((HEAD_DIM, T), F32)
        for j in range(qi + 1):
            p = jnp.exp2(scores[j] - m_run)
            l_run = l_run + jnp.sum(p, axis=0, keepdims=True)
            acc = acc + _dot_tn(rows(v_ref, j), p.astype(BF16))
        o_ref[0, qi * T:(qi + 1) * T, :] = (acc / l_run).T.astype(o_ref.dtype)

    cur = score_stage(0)
    for qi in range(n_blocks):
        nxt = score_stage(qi + 1) if qi + 1 < n_blocks else None
        value_stage(qi, *cur)
        cur = nxt


def _moba_attention(qkv, n_heads):
    B, S, _ = qkv.shape
    H = n_heads
    assert S % MOBA_BLOCK == 0
    nb = S // MOBA_BLOCK
    gate_rows = -(-nb // BF16_SUBLANES) * BF16_SUBLANES
    in_specs, out_spec = _head_specs(S, H)
    return pl.pallas_call(
        functools.partial(_moba_kernel, n_heads=H),
        grid=(B, H),
        in_specs=in_specs,
        out_specs=out_spec,
        out_shape=jax.ShapeDtypeStruct((B, S, H * HEAD_DIM), BF16),
        scratch_shapes=[pltpu.VMEM((gate_rows, HEAD_DIM), F32)],
        compiler_params=_params("parallel", "parallel"),
        name="moba_attn",
    )(qkv, qkv, qkv)


def _outproj_kernel(o_ref, w_ref, x_ref, ga_ref, gn_ref, sc_ref, sh_ref, *rest, n_experts):
    if n_experts:
        wr_ref, x1_ref, h_ref, route_ref = rest
    else:
        x1_ref, h_ref = rest
    x1 = x_ref[...] + ga_ref[0, 0] * _dot(o_ref[...], w_ref[...])
    x1_ref[...] = x1
    h = _norm_mod(x1, gn_ref[...], sc_ref[0, 0], sh_ref[0, 0])
    h_ref[...] = h.astype(h_ref.dtype)
    if n_experts:
        h_hi, h_lo = _split_bf16(h)
        both = _dot(h_hi, wr_ref[...])
        logits = both[:, :LANES] + both[:, LANES:] + _dot(h_lo, wr_ref[:, :LANES])
        lane = lax.broadcasted_iota(jnp.int32, logits.shape, 1).astype(F32)
        lg = jnp.where(lane < n_experts, logits, -jnp.inf)
        v1 = jnp.max(lg, axis=-1, keepdims=True)
        i1 = jnp.min(jnp.where(lg == v1, lane, float(LANES)), axis=-1, keepdims=True)
        lg2 = jnp.where(lane == i1, -jnp.inf, lg)
        v2 = jnp.max(lg2, axis=-1, keepdims=True)
        i2 = jnp.min(jnp.where(lg2 == v2, lane, float(LANES)), axis=-1, keepdims=True)
        e2 = jnp.exp(v2 - v1)
        w1 = 1.0 / (1.0 + e2)
        w2 = e2 / (1.0 + e2)
        route_ref[...] = jnp.where(
            lane == 0.0, i1,
            jnp.where(lane == 1.0, i2, jnp.where(lane == 2.0, w1, jnp.where(lane == 3.0, w2, 0.0))))


def _outproj(o2d, w_bf16, x2d, gain, mod_l, seq, h_dtype, router_w=None):
    N, D = x2d.shape
    tm = _tile(seq, 256)
    rows_per_batch = seq // tm
    mod_spec = lambda k: pl.BlockSpec((1, 1, 1, D), lambda i: (i // rows_per_batch, k, 0, 0))
    row_spec = pl.BlockSpec((tm, D), lambda i: (i, 0))
    in_specs = [
        row_spec,
        pl.BlockSpec((D, D), lambda i: (0, 0)),
        row_spec,
        mod_spec(2), pl.BlockSpec((1, D), lambda i: (0, 0)), mod_spec(4), mod_spec(3),
    ]
    args = [o2d, w_bf16, x2d, mod_l, gain.reshape(1, D), mod_l, mod_l]
    out_specs = [row_spec, row_spec]
    out_shape = [jax.ShapeDtypeStruct((N, D), F32), jax.ShapeDtypeStruct((N, D), h_dtype)]
    n_experts = 0
    if router_w is not None:
        n_experts = router_w.shape[1]
        assert n_experts <= LANES
        wr = jnp.zeros((D, LANES), F32).at[:, :n_experts].set(router_w)
        wr_hi = wr.astype(BF16)
        wr_lo = (wr - wr_hi.astype(F32)).astype(BF16)
        in_specs.append(pl.BlockSpec((D, 2 * LANES), lambda i: (0, 0)))
        args.append(jnp.concatenate([wr_hi, wr_lo], axis=1))
        out_specs.append(pl.BlockSpec((tm, LANES), lambda i: (i, 0)))
        out_shape.append(jax.ShapeDtypeStruct((N, LANES), F32))
    return pl.pallas_call(
        functools.partial(_outproj_kernel, n_experts=n_experts),
        grid=(N // tm,),
        in_specs=in_specs,
        out_specs=out_specs,
        out_shape=out_shape,
        compiler_params=_params("parallel"),
        name="outproj_route" if n_experts else "outproj",
    )(*args)


def _swiglu_act(h, wg, wu):
    g = _dot(h, wg)
    u = _dot(h, wu)
    return (g * (1.0 / (1.0 + jnp.exp(-g))) * u).astype(BF16)


def _ffn_kernel(h_ref, wg_ref, wu_ref, wo_ref, x_ref, gf_ref, out_ref, acc_scr):
    j = pl.program_id(1)

    @pl.when(j == 0)
    def _():
        acc_scr[...] = jnp.zeros_like(acc_scr)

    acc_scr[...] += _dot(_swiglu_act(h_ref[...], wg_ref[...], wu_ref[...]), wo_ref[...])

    @pl.when(j == pl.num_programs(1) - 1)
    def _():
        out_ref[...] = x_ref[...] + gf_ref[0, 0] * acc_scr[...]


def _dense_ffn(h_bf16, w_in_bf16, w_out_bf16, x2d, mod_l, seq):
    N, D = x2d.shape
    d_ff = w_out_bf16.shape[0]
    tm = _tile(seq, 512)
    tn = _tile(d_ff, 512)
    nj = d_ff // tn
    rows_per_batch = seq // tm
    row_spec = pl.BlockSpec((tm, D), lambda i, j: (i, 0))
    return pl.pallas_call(
        _ffn_kernel,
        grid=(N // tm, nj),
        in_specs=[
            row_spec,
            pl.BlockSpec((D, tn), lambda i, j: (0, j)),
            pl.BlockSpec((D, tn), lambda i, j: (0, j + nj)),
            pl.BlockSpec((tn, D), lambda i, j: (j, 0)),
            row_spec,
            pl.BlockSpec((1, 1, 1, D), lambda i, j: (i // rows_per_batch, 5, 0, 0)),
        ],
        out_specs=row_spec,
        out_shape=jax.ShapeDtypeStruct((N, D), F32),
        scratch_shapes=[pltpu.VMEM((tm, D), F32)],
        compiler_params=_params("parallel", "arbitrary"),
        name="dense_ffn",
    )(h_bf16, w_in_bf16, w_in_bf16, w_out_bf16, x2d, mod_l)


ROW_GROUP = 8


def _moe_kernel(te_ref, nv_ref, rt_ref, h_hbm, wg_ref, wu_ref, wo_ref, y_hbm,
                rows_scr, stage_scr, wg_scr, wu_scr, wo_scr, acc_scr, sem, out_sem, *, sub):
    t = pl.program_id(0)
    j = pl.program_id(1)
    n_valid = nv_ref[t]
    tile_rows = rows_scr.shape[0]
    half = sub // 2
    chunk = stage_scr.shape[1] * ROW_GROUP
    assert chunk == half
    n_chunks = tile_rows // chunk
    valid_chunks = n_valid

    def out_copy(tile):
        return pltpu.make_async_copy(acc_scr, y_hbm.at[pl.ds(tile * tile_rows, tile_rows), :], out_sem)

    def row_copy(s, group, u):
        tok = rt_ref[t * tile_rows + s * chunk + group * ROW_GROUP + u]
        return pltpu.make_async_copy(h_hbm.at[pl.ds(tok, 1), :],
                                     stage_scr.at[s % 2, group, pl.ds(u, 1), :], sem.at[s % 2])

    def issue(s):
        def body(group, c):
            for u in range(ROW_GROUP):
                row_copy(s, group, u).start()
            return c
        lax.fori_loop(0, chunk // ROW_GROUP, body, 0)

    def drain(s):
        def body(group, c):
            for u in range(ROW_GROUP):
                row_copy(s, group, u).wait()
            return c
        lax.fori_loop(0, chunk // ROW_GROUP, body, 0)
        staged = stage_scr[s % 2].reshape(chunk, stage_scr.shape[3])
        rows_scr[s * chunk:(s + 1) * chunk, :] = staged.astype(BF16)

    @pl.when(j == 0)
    def _():
        pl.when(valid_chunks > 0)(functools.partial(issue, 0))
        for s in range(n_chunks):
            if s + 1 < n_chunks:
                pl.when(s + 1 < valid_chunks)(functools.partial(issue, s + 1))
            pl.when(s < valid_chunks)(functools.partial(drain, s))
        pl.when(t > 0)(lambda: out_copy(t - 1).wait())
        acc_scr[...] = jnp.zeros_like(acc_scr)

    def sub_tile(r0, n_rows, wg, wu, wo):
        act = _swiglu_act(rows_scr[pl.ds(r0, n_rows), :], wg, wu)
        acc_scr[pl.ds(r0, n_rows), :] += _dot(act, wo)

    def from_scratch(r0, n_rows):
        sub_tile(pl.multiple_of(r0, half), n_rows, wg_scr[...], wu_scr[...], wo_scr[...])

    def cast_weights():
        wg = wg_ref[0].astype(BF16)
        wu = wu_ref[0].astype(BF16)
        wo = wo_ref[0].astype(BF16)
        wg_scr[...] = wg
        wu_scr[...] = wu
        wo_scr[...] = wo
        return wg, wu, wo

    n_full = lax.shift_right_logical(n_valid, 1)
    has_half = (n_valid & 1) == 1

    @pl.when(n_full > 0)
    def _():
        sub_tile(0, sub, *cast_weights())

    @pl.when((n_full == 0) & has_half)
    def _():
        cast_weights()

    n_rest = jnp.maximum(n_full - 1, 0)
    n_pairs = lax.shift_right_logical(n_rest, 1)

    def pair(p, c):
        r0 = sub + p * (2 * sub)
        from_scratch(r0, sub)
        from_scratch(r0 + sub, sub)
        return c

    lax.fori_loop(0, n_pairs, pair, 0)
    pl.when((n_rest & 1) == 1)(lambda: from_scratch(sub + n_pairs * (2 * sub), sub))
    pl.when(has_half)(lambda: from_scratch(n_full * sub, half))

    @pl.when(j == pl.num_programs(1) - 1)
    def _():
        out_copy(t).start()
        pl.when(t == pl.num_programs(0) - 1)(lambda: out_copy(t).wait())


def _moe_grouped(h_f32, w_in, w_out, tile_expert, tile_nvalid, row_token, tile_rows, sub):
    N, D = h_f32.shape
    E, d_ff, _ = w_out.shape
    n_tiles = tile_expert.shape[0]
    tn = _tile(d_ff, 256)
    nj = d_ff // tn

    def ff_index(t, j, nv):
        return jnp.where(nv[t] > 0, j, nj - 1)

    assert sub % 16 == 0
    chunk = sub // 2
    return pl.pallas_call(
        functools.partial(_moe_kernel, sub=sub),
        grid_spec=pltpu.PrefetchScalarGridSpec(
            num_scalar_prefetch=3,
            grid=(n_tiles, nj),
            in_specs=[
                pl.BlockSpec(memory_space=pl.ANY),
                pl.BlockSpec((1, D, tn), lambda t, j, te, nv, rt: (te[t], 0, ff_index(t, j, nv))),
                pl.BlockSpec((1, D, tn), lambda t, j, te, nv, rt: (te[t], 0, ff_index(t, j, nv) + nj)),
                pl.BlockSpec((1, tn, D), lambda t, j, te, nv, rt: (te[t], ff_index(t, j, nv), 0)),
            ],
            out_specs=pl.BlockSpec(memory_space=pl.ANY),
            scratch_shapes=[
                pltpu.VMEM((tile_rows, D), BF16),
                pltpu.VMEM((2, chunk // ROW_GROUP, ROW_GROUP, D), F32),
                pltpu.VMEM((D, tn), BF16),
                pltpu.VMEM((D, tn), BF16),
                pltpu.VMEM((tn, D), BF16),
                pltpu.VMEM((tile_rows, D), F32),
                pltpu.SemaphoreType.DMA((2,)),
                pltpu.SemaphoreType.DMA(()),
            ],
        ),
        out_shape=jax.ShapeDtypeStruct((n_tiles * tile_rows, D), F32),
        compiler_params=_params("arbitrary", "arbitrary"),
        name="moe_grouped",
    )(tile_expert, tile_nvalid, row_token, h_f32, w_in, w_in, w_out)


def _combine_kernel(p0_ref, p1_ref, ys_hbm, x_ref, gf_ref, route_ref, out_ref, buf0, buf1, sem0, sem1):
    i = pl.program_id(0)
    n_groups, _, D = buf0.shape
    tm = n_groups * ROW_GROUP

    def copies(group, u):
        r = i * tm + group * ROW_GROUP + u
        c0 = pltpu.make_async_copy(ys_hbm.at[pl.ds(p0_ref[r], 1), :], buf0.at[group, pl.ds(u, 1), :], sem0)
        c1 = pltpu.make_async_copy(ys_hbm.at[pl.ds(p1_ref[r], 1), :], buf1.at[group, pl.ds(u, 1), :], sem1)
        return c0, c1

    def start(group, c):
        for u in range(ROW_GROUP):
            c0, c1 = copies(group, u)
            c0.start(priority=0)
            c1.start(priority=1)
        return c

    lax.fori_loop(0, n_groups, start, 0)

    def wait(group, c):
        for u in range(ROW_GROUP):
            c0, c1 = copies(group, u)
            c0.wait()
            c1.wait()
        return c

    lax.fori_loop(0, n_groups, wait, 0)
    w0 = route_ref[:, MOE_TOPK:MOE_TOPK + 1]
    w1 = route_ref[:, MOE_TOPK + 1:MOE_TOPK + 2]
    y0 = buf0[...].reshape(tm, D)
    y1 = buf1[...].reshape(tm, D)
    out_ref[...] = x_ref[...] + gf_ref[0, 0] * (w0 * y0 + w1 * y1)


def _moe_combine(y_sorted, pos0, pos1, x2d, mod_l, route, seq):
    N, D = x2d.shape
    tm = _tile(seq, 256)
    rows_per_batch = seq // tm
    return pl.pallas_call(
        _combine_kernel,
        grid_spec=pltpu.PrefetchScalarGridSpec(
            num_scalar_prefetch=2,
            grid=(N // tm,),
            in_specs=[
                pl.BlockSpec(memory_space=pl.ANY),
                pl.BlockSpec((tm, D), lambda i, p0, p1: (i, 0)),
                pl.BlockSpec((1, 1, 1, D), lambda i, p0, p1: (i // rows_per_batch, 5, 0, 0)),
                pl.BlockSpec((tm, LANES), lambda i, p0, p1: (i, 0)),
            ],
            out_specs=pl.BlockSpec((tm, D), lambda i, p0, p1: (i, 0)),
            scratch_shapes=[
                pltpu.VMEM((tm // ROW_GROUP, ROW_GROUP, D), F32),
                pltpu.VMEM((tm // ROW_GROUP, ROW_GROUP, D), F32),
                pltpu.SemaphoreType.DMA(()),
                pltpu.SemaphoreType.DMA(()),
            ],
        ),
        out_shape=jax.ShapeDtypeStruct((N, D), F32),
        compiler_params=_params("arbitrary"),
        name="moe_combine",
    )(pos0, pos1, y_sorted, x2d, mod_l, route)


def _moe_tiling(n_assign, n_experts):
    per_expert = n_assign // n_experts
    sub = _tile(per_expert, 512)
    tile_rows = sub * (-(-(per_expert // 2 + sub) // sub))
    n_tiles = n_assign // tile_rows + n_experts
    return sub, tile_rows, n_tiles


def _routing_tables(route, n_experts, sub, tile_rows, n_tiles):
    N = route.shape[0]
    A = N * MOE_TOPK
    expert = route[:, :MOE_TOPK].astype(jnp.int32).reshape(A)
    onehot = (expert[:, None] == jnp.arange(n_experts, dtype=jnp.int32)[None, :]).astype(jnp.int32)
    csum = jnp.cumsum(onehot, axis=0)
    rank = jnp.sum(onehot * csum, axis=1) - 1
    counts = csum[-1]
    tiles_per = (counts + tile_rows - 1) // tile_rows
    tile_end = jnp.cumsum(tiles_per)
    tile_start = tile_end - tiles_per
    dest = tile_start[expert] * tile_rows + rank
    row_token = jnp.zeros((n_tiles * tile_rows,), jnp.int32).at[dest].set(
        jnp.arange(A, dtype=jnp.int32) // MOE_TOPK, unique_indices=True)
    used = tile_end[-1]
    tix = jnp.arange(n_tiles, dtype=jnp.int32)
    clipped = jnp.minimum(tix, used - 1)
    tile_expert = jnp.sum((clipped[:, None] >= tile_end[None, :]).astype(jnp.int32), axis=1)
    tile_expert = jnp.minimum(tile_expert, n_experts - 1).astype(jnp.int32)
    rows_left = counts[tile_expert] - (tix - tile_start[tile_expert]) * tile_rows
    half = sub // 2
    n_valid = jnp.clip((rows_left + half - 1) // half, 0, tile_rows // half)
    tile_nvalid = jnp.where(tix < used, n_valid, 0).astype(jnp.int32)
    dest2 = dest.reshape(N, MOE_TOPK)
    return tile_expert, tile_nvalid, row_token, dest2[:, 0], dest2[:, 1]


def kernel(x, c, w_ada, b_ada, attn_norm, ffn_norm, w_qkv, w_o, qk_norm_q, qk_norm_k,
           ffn_w_in, ffn_w_out, router_w, moe_w_in, moe_w_out):
    B, S, D = x.shape
    depth = w_ada.shape[0]
    H = D // HEAD_DIM
    N = B * S
    mod = _adaln(c, w_ada, b_ada).reshape(depth, B, 6, 1, D)
    x2d = x.reshape(N, D)
    for i in range(depth):
        mod_l = mod[i]
        wqkv = w_qkv[i].astype(BF16)
        wo = w_o[i].astype(BF16)
        if i % 2 == 0:
            qkv = _qkv_proj(x2d, attn_norm[i], mod_l, wqkv, S, H)
            o = _stickbreak_attention(qkv.reshape(B, S, 3 * D), H)
            x2d, h = _outproj(o.reshape(N, D), wo, x2d, ffn_norm[i], mod_l, S, BF16)
            x2d = _dense_ffn(h, ffn_w_in[i // 2].astype(BF16), ffn_w_out[i // 2].astype(BF16),
                             x2d, mod_l, S)
        else:
            jj = i // 2
            qkv = _qkv_proj(x2d, attn_norm[i], mod_l, wqkv, S, H, (qk_norm_q[jj], qk_norm_k[jj]))
            o = _moba_attention(qkv.reshape(B, S, 3 * D), H)
            x2d, h, route = _outproj(o.reshape(N, D), wo, x2d, ffn_norm[i], mod_l, S, F32,
                                     router_w[jj])
            n_experts = router_w.shape[-1]
            sub, tile_rows, n_tiles = _moe_tiling(N * MOE_TOPK, n_experts)
            te, nv, row_token, pos0, pos1 = _routing_tables(route, n_experts, sub, tile_rows, n_tiles)
            y_sorted = _moe_grouped(h, moe_w_in[jj], moe_w_out[jj], te, nv, row_token, tile_rows, sub)
            x2d = _moe_combine(y_sorted, pos0, pos1, x2d, mod_l, route, S)
    return x2d.reshape(B, S, D)
```

```python
import functools

import jax
import jax.numpy as jnp
from jax import lax
from jax.experimental import pallas as pl
from jax.experimental.pallas import tpu as pltpu

F32 = jnp.float32
BF16 = jnp.bfloat16

HEAD_DIM = 128
MOBA_BLOCK = 256
MOBA_TOPK = 3
MOE_TOPK = 2
EPS = 1e-6
MASKED = -1e30
LANES = 128
BF16_SUBLANES = 16
VMEM_LIMIT = 56 * 1024 * 1024


def _tile(dim, pref):
    t = min(dim, pref)
    while dim % t:
        t //= 2
    return t


def _params(*sem):
    return pltpu.CompilerParams(dimension_semantics=sem, vmem_limit_bytes=VMEM_LIMIT)


def _dot(a, b):
    return jnp.dot(a, b, preferred_element_type=F32)


def _dot_nt(a, b):
    return lax.dot_general(a, b, (((1,), (1,)), ((), ())), preferred_element_type=F32)


def _split_bf16(x):
    hi = x.astype(BF16)
    lo = (x - hi.astype(F32)).astype(BF16)
    return hi, lo


def _adaln_kernel(c_ref, w_ref, b_ref, o_ref):
    acc = _dot(c_ref[...].astype(BF16), w_ref[0].astype(BF16))
    o_ref[0] = acc + b_ref[0]


def _adaln(c, w_ada, b_ada):
    L, D, D6 = w_ada.shape
    B = c.shape[0]
    tn = _tile(D6, 1024)
    return pl.pallas_call(
        _adaln_kernel,
        grid=(L, D6 // tn),
        in_specs=[
            pl.BlockSpec((B, D), lambda l, j: (0, 0)),
            pl.BlockSpec((1, D, tn), lambda l, j: (l, 0, j)),
            pl.BlockSpec((1, 1, tn), lambda l, j: (l, 0, j)),
        ],
        out_specs=pl.BlockSpec((1, B, tn), lambda l, j: (l, 0, j)),
        out_shape=jax.ShapeDtypeStruct((L, B, D6), F32),
        compiler_params=_params("parallel", "parallel"),
        name="adaln",
    )(c, w_ada, b_ada.reshape(L, 1, D6))


def _norm_mod(x, gain, scale, shift):
    ms = jnp.mean(x * x, axis=-1, keepdims=True)
    y = x * lax.rsqrt(ms + EPS) * gain
    return y * (1.0 + scale) + shift


def _qkv_kernel(x_ref, g_ref, sc_ref, sh_ref, w_ref, col_ref, o_ref, h_scr, *, qk_norm, d_model):
    j = pl.program_id(1)
    tn = o_ref.shape[1]

    @pl.when(j == 0)
    def _():
        h = _norm_mod(x_ref[...], g_ref[...], sc_ref[0, 0], sh_ref[0, 0])
        h_scr[...] = h.astype(BF16)

    acc = _dot(h_scr[...], w_ref[...])
    if not qk_norm:
        o_ref[...] = (acc * col_ref[...]).astype(o_ref.dtype)
    else:
        is_qk = j * tn < 2 * d_model

        @pl.when(is_qk)
        def _():
            for hh in range(tn // HEAD_DIM):
                sl = slice(hh * HEAD_DIM, (hh + 1) * HEAD_DIM)
                blk = acc[:, sl]
                ms = jnp.mean(blk * blk, axis=-1, keepdims=True)
                o_ref[:, sl] = (blk * lax.rsqrt(ms + EPS) * col_ref[:, sl]).astype(o_ref.dtype)

        @pl.when(jnp.logical_not(is_qk))
        def _():
            o_ref[...] = acc.astype(o_ref.dtype)


Q_FOLD = HEAD_DIM ** -0.5 * 1.4426950408889634


def _qkv_proj(x2d, gain, mod_l, w_bf16, seq, n_heads, qk_gains=None):
    N, D = x2d.shape
    D3 = w_bf16.shape[1]
    tm = _tile(seq, 1024)
    tn = _tile(D, 1024)
    assert (2 * D) % tn == 0 and tn % HEAD_DIM == 0
    rows_per_batch = seq // tm
    qk_norm = qk_gains is not None
    gq, gk = qk_gains if qk_norm else (jnp.ones((HEAD_DIM,), F32),) * 2
    col = jnp.concatenate([jnp.tile(gq, n_heads) * Q_FOLD, jnp.tile(gk, n_heads),
                           jnp.ones((D,), F32)]).reshape(1, D3)
    in_specs = [
        pl.BlockSpec((tm, D), lambda i, j: (i, 0)),
        pl.BlockSpec((1, D), lambda i, j: (0, 0)),
        pl.BlockSpec((1, 1, 1, D), lambda i, j: (i // rows_per_batch, 1, 0, 0)),
        pl.BlockSpec((1, 1, 1, D), lambda i, j: (i // rows_per_batch, 0, 0, 0)),
        pl.BlockSpec((D, tn), lambda i, j: (0, j)),
        pl.BlockSpec((1, tn), lambda i, j: (0, j)),
    ]
    args = [x2d, gain.reshape(1, D), mod_l, mod_l, w_bf16, col]
    return pl.pallas_call(
        functools.partial(_qkv_kernel, qk_norm=qk_norm, d_model=D),
        grid=(N // tm, D3 // tn),
        in_specs=in_specs,
        out_specs=pl.BlockSpec((tm, tn), lambda i, j: (i, j)),
        out_shape=jax.ShapeDtypeStruct((N, D3), BF16),
        scratch_shapes=[pltpu.VMEM((tm, D), BF16)],
        compiler_params=_params("parallel", "arbitrary"),
        name="qkv_norm" if qk_norm else "qkv",
    )(*args)


def _dot_tn(a, b):
    return lax.dot_general(a, b, (((0,), (0,)), ((), ())), preferred_element_type=F32)


def _sb_kernel(q_ref, k_ref, v_ref, o_ref, *, tq):
    n_qblocks = q_ref.shape[1] // tq
    key = lax.broadcasted_iota(jnp.int32, (tq, tq), 0)
    qry = lax.broadcasted_iota(jnp.int32, (tq, tq), 1)
    past = key < qry
    later_keys = jnp.where(qry > key, 1.0, 0.0).astype(BF16)

    def rows(ref, i):
        return ref[0, i * tq:(i + 1) * tq, :]

    def score_stage(qi):
        q = rows(q_ref, qi)
        out = []
        for j in range(qi + 1):
            z = _dot_nt(rows(k_ref, j), q)
            log_beta = jnp.minimum(z, 0.0) - jnp.log2(1.0 + jnp.exp2(-jnp.abs(z)))
            log_keep = log_beta - z
            if j == qi:
                log_keep = jnp.where(past, log_keep, 0.0)
            out.append((log_beta, log_keep))
        return out

    def cumsum_stage(parts):
        return [_dot(later_keys, log_keep.astype(BF16)) for (_, log_keep) in parts]

    def value_stage(qi, parts, laters):
        carry = jnp.zeros((1, tq), F32)
        acc = jnp.zeros((HEAD_DIM, tq), F32)
        for j in range(qi, -1, -1):
            log_beta, log_keep = parts[j]
            a = jnp.exp2(log_beta + laters[j] + carry)
            if j == qi:
                a = jnp.where(past, a, 0.0)
            acc = acc + _dot_tn(rows(v_ref, j), a.astype(BF16))
            carry = carry + laters[j][0:1, :] + log_keep[0:1, :]
        o_ref[0, qi * tq:(qi + 1) * tq, :] = acc.T.astype(o_ref.dtype)

    parts = score_stage(0)
    for qi in range(n_qblocks):
        laters = cumsum_stage(parts)
        nxt = score_stage(qi + 1) if qi + 1 < n_qblocks else None
        value_stage(qi, parts, laters)
        parts = nxt


def _head_specs(S, H):
    spec = lambda col: pl.BlockSpec((1, S, HEAD_DIM), lambda b, h: (b, 0, col * H + h))
    return [spec(0), spec(1), spec(2)], spec(0)


def _stickbreak_attention(qkv, n_heads):
    B, S, _ = qkv.shape
    H = n_heads
    in_specs, out_spec = _head_specs(S, H)
    return pl.pallas_call(
        functools.partial(_sb_kernel, tq=_tile(S, 256)),
        grid=(B, H),
        in_specs=in_specs,
        out_specs=out_spec,
        out_shape=jax.ShapeDtypeStruct((B, S, H * HEAD_DIM), BF16),
        compiler_params=_params("parallel", "parallel"),
        name="stickbreak_attn",
    )(qkv, qkv, qkv)


def _moba_kernel(q_ref, k_ref, v_ref, o_ref, kmean_scr, *, n_heads):
    h = pl.program_id(1)
    T = MOBA_BLOCK
    n_blocks = q_ref.shape[1] // T

    def rows(ref, i):
        return ref[0, i * T:(i + 1) * T, :]

    kmean_scr[...] = jnp.zeros_like(kmean_scr)
    for n in range(n_blocks):
        kmean_scr[n:n + 1, :] = jnp.mean(rows(k_ref, n).astype(F32), axis=0, keepdims=True)
    km_hi, km_lo = _split_bf16(kmean_scr[...])
    blk = lax.broadcasted_iota(jnp.int32, (kmean_scr.shape[0], T), 0)
    slope = jnp.exp2(jnp.full((1, 1), h + 1, jnp.int32).astype(F32) * (-8.0 / n_heads))
    slope2 = slope * 1.4426950408889634
    rel = (lax.broadcasted_iota(jnp.int32, (T, T), 1)
           - lax.broadcasted_iota(jnp.int32, (T, T), 0)).astype(F32)
    alibi = [slope2 * (rel + float(d * T)) for d in range(n_blocks)]

    def choose(qi, q):
        gate = _dot_nt(km_hi, q) + _dot_nt(km_lo, q)
        fully_past = blk < qi
        g = jnp.where(fully_past, gate, -jnp.inf)
        beaten = jnp.zeros(gate.shape, F32)
        for m in range(qi):
            gm = g[m:m + 1, :]
            first = jnp.where(blk > m, 1.0, 0.0)
            beaten = beaten + jnp.where(gm > g, 1.0, jnp.where(gm == g, first, 0.0))
        return jnp.where(fully_past, jnp.where(beaten < MOBA_TOPK, 1.0, 0.0), 0.0)

    def score_stage(qi):
        q = rows(q_ref, qi)
        chosen = choose(qi, q)
        scores = []
        m_run = None
        for j in range(qi + 1):
            s = _dot_nt(rows(k_ref, j), q) - alibi[qi - j]
            if j == qi:
                s = jnp.where(rel >= 0.0, s, MASKED)
            else:
                s = jnp.where(chosen[j:j + 1, :] > 0.0, s, MASKED)
            scores.append(s)
            m_j = jnp.max(s, axis=0, keepdims=True)
            m_run = m_j if m_run is None else jnp.maximum(m_run, m_j)
        return scores, m_run

    def value_stage(qi, scores, m_run):
        l_run = jnp.zeros((1, T), F32)
        acc = jnp.zeros((HEAD_DIM, T), F32)
        for j in range(qi + 1):
            p = jnp.exp2(scores[j] - m_run)
            l_run = l_run + jnp.sum(p, axis=0, keepdims=True)
            acc = acc + _dot_tn(rows(v_ref, j), p.astype(BF16))
        o_ref[0, qi * T:(qi + 1) * T, :] = (acc / l_run).T.astype(o_ref.dtype)

    cur = score_stage(0)
    for qi in range(n_blocks):
        nxt = score_stage(qi + 1) if qi + 1 < n_blocks else None
        value_stage(qi, *cur)
        cur = nxt


def _moba_attention(qkv, n_heads):
    B, S, _ = qkv.shape
    H = n_heads
    assert S % MOBA_BLOCK == 0
    nb = S // MOBA_BLOCK
    gate_rows = -(-nb // BF16_SUBLANES) * BF16_SUBLANES
    in_specs, out_spec = _head_specs(S, H)
    return pl.pallas_call(
        functools.partial(_moba_kernel, n_heads=H),
        grid=(B, H),
        in_specs=in_specs,
        out_specs=out_spec,
        out_shape=jax.ShapeDtypeStruct((B, S, H * HEAD_DIM), BF16),
        scratch_shapes=[pltpu.VMEM((gate_rows, HEAD_DIM), F32)],
        compiler_params=_params("parallel", "parallel"),
        name="moba_attn",
    )(qkv, qkv, qkv)


def _outproj_kernel(o_ref, w_ref, x_ref, ga_ref, gn_ref, sc_ref, sh_ref, *rest, n_experts):
    if n_experts:
        wr_ref, x1_ref, h_ref, route_ref = rest
    else:
        x1_ref, h_ref = rest
    x1 = x_ref[...] + ga_ref[0, 0] * _dot(o_ref[...], w_ref[...])
    x1_ref[...] = x1
    h = _norm_mod(x1, gn_ref[...], sc_ref[0, 0], sh_ref[0, 0])
    h_ref[...] = h.astype(h_ref.dtype)
    if n_experts:
        h_hi, h_lo = _split_bf16(h)
        both = _dot(h_hi, wr_ref[...])
        logits = both[:, :LANES] + both[:, LANES:] + _dot(h_lo, wr_ref[:, :LANES])
        lane = lax.broadcasted_iota(jnp.int32, logits.shape, 1).astype(F32)
        lg = jnp.where(lane < n_experts, logits, -jnp.inf)
        v1 = jnp.max(lg, axis=-1, keepdims=True)
        i1 = jnp.min(jnp.where(lg == v1, lane, float(LANES)), axis=-1, keepdims=True)
        lg2 = jnp.where(lane == i1, -jnp.inf, lg)
        v2 = jnp.max(lg2, axis=-1, keepdims=True)
        i2 = jnp.min(jnp.where(lg2 == v2, lane, float(LANES)), axis=-1, keepdims=True)
        e2 = jnp.exp(v2 - v1)
        w1 = 1.0 / (1.0 + e2)
        w2 = e2 / (1.0 + e2)
        route_ref[...] = jnp.where(
            lane == 0.0, i1,
            jnp.where(lane == 1.0, i2, jnp.where(lane == 2.0, w1, jnp.where(lane == 3.0, w2, 0.0))))


def _outproj(o2d, w_bf16, x2d, gain, mod_l, seq, h_dtype, router_w=None):
    N, D = x2d.shape
    tm = _tile(seq, 256)
    rows_per_batch = seq // tm
    mod_spec = lambda k: pl.BlockSpec((1, 1, 1, D), lambda i: (i // rows_per_batch, k, 0, 0))
    row_spec = pl.BlockSpec((tm, D), lambda i: (i, 0))
    in_specs = [
        row_spec,
        pl.BlockSpec((D, D), lambda i: (0, 0)),
        row_spec,
        mod_spec(2), pl.BlockSpec((1, D), lambda i: (0, 0)), mod_spec(4), mod_spec(3),
    ]
    args = [o2d, w_bf16, x2d, mod_l, gain.reshape(1, D), mod_l, mod_l]
    out_specs = [row_spec, row_spec]
    out_shape = [jax.ShapeDtypeStruct((N, D), F32), jax.ShapeDtypeStruct((N, D), h_dtype)]
    n_experts = 0
    if router_w is not None:
        n_experts = router_w.shape[1]
        assert n_experts <= LANES
        wr = jnp.zeros((D, LANES), F32).at[:, :n_experts].set(router_w)
        wr_hi = wr.astype(BF16)
        wr_lo = (wr - wr_hi.astype(F32)).astype(BF16)
        in_specs.append(pl.BlockSpec((D, 2 * LANES), lambda i: (0, 0)))
        args.append(jnp.concatenate([wr_hi, wr_lo], axis=1))
        out_specs.append(pl.BlockSpec((tm, LANES), lambda i: (i, 0)))
        out_shape.append(jax.ShapeDtypeStruct((N, LANES), F32))
    return pl.pallas_call(
        functools.partial(_outproj_kernel, n_experts=n_experts),
        grid=(N // tm,),
        in_specs=in_specs,
        out_specs=out_specs,
        out_shape=out_shape,
        compiler_params=_params("parallel"),
        name="outproj_route" if n_experts else "outproj",
    )(*args)


def _swiglu_act(h, wg, wu):
    g = _dot(h, wg)
    u = _dot(h, wu)
    return (g * (1.0 / (1.0 + jnp.exp(-g))) * u).astype(BF16)


def _ffn_kernel(h_ref, wg_ref, wu_ref, wo_ref, x_ref, gf_ref, out_ref, acc_scr):
    j = pl.program_id(1)

    @pl.when(j == 0)
    def _():
        acc_scr[...] = jnp.zeros_like(acc_scr)

    acc_scr[...] += _dot(_swiglu_act(h_ref[...], wg_ref[...], wu_ref[...]), wo_ref[...])

    @pl.when(j == pl.num_programs(1) - 1)
    def _():
        out_ref[...] = x_ref[...] + gf_ref[0, 0] * acc_scr[...]


def _dense_ffn(h_bf16, w_in_bf16, w_out_bf16, x2d, mod_l, seq):
    N, D = x2d.shape
    d_ff = w_out_bf16.shape[0]
    tm = _tile(seq, 512)
    tn = _tile(d_ff, 512)
    nj = d_ff // tn
    rows_per_batch = seq // tm
    row_spec = pl.BlockSpec((tm, D), lambda i, j: (i, 0))
    return pl.pallas_call(
        _ffn_kernel,
        grid=(N // tm, nj),
        in_specs=[
            row_spec,
            pl.BlockSpec((D, tn), lambda i, j: (0, j)),
            pl.BlockSpec((D, tn), lambda i, j: (0, j + nj)),
            pl.BlockSpec((tn, D), lambda i, j: (j, 0)),
            row_spec,
            pl.BlockSpec((1, 1, 1, D), lambda i, j: (i // rows_per_batch, 5, 0, 0)),
        ],
        out_specs=row_spec,
        out_shape=jax.ShapeDtypeStruct((N, D), F32),
        scratch_shapes=[pltpu.VMEM((tm, D), F32)],
        compiler_params=_params("parallel", "arbitrary"),
        name="dense_ffn",
    )(h_bf16, w_in_bf16, w_in_bf16, w_out_bf16, x2d, mod_l)


ROW_GROUP = 8


def _moe_kernel(te_ref, nv_ref, rt_ref, h_hbm, wg_ref, wu_ref, wo_ref, y_hbm,
                rows_scr, stage_scr, wg_scr, wu_scr, wo_scr, acc_scr, sem, out_sem, *, sub):
    t = pl.program_id(0)
    j = pl.program_id(1)
    n_valid = nv_ref[t]
    tile_rows = rows_scr.shape[0]
    half = sub // 2
    chunk = stage_scr.shape[1] * ROW_GROUP
    assert chunk == half
    n_chunks = tile_rows // chunk
    valid_chunks = n_valid

    def out_copy(tile):
        return pltpu.make_async_copy(acc_scr, y_hbm.at[pl.ds(tile * tile_rows, tile_rows), :], out_sem)

    def row_copy(s, group, u):
        tok = rt_ref[t * tile_rows + s * chunk + group * ROW_GROUP + u]
        return pltpu.make_async_copy(h_hbm.at[pl.ds(tok, 1), :],
                                     stage_scr.at[s % 2, group, pl.ds(u, 1), :], sem.at[s % 2])

    def issue(s):
        def body(group, c):
            for u in range(ROW_GROUP):
                row_copy(s, group, u).start()
            return c
        lax.fori_loop(0, chunk // ROW_GROUP, body, 0)

    def drain(s):
        def body(group, c):
            for u in range(ROW_GROUP):
                row_copy(s, group, u).wait()
            return c
        lax.fori_loop(0, chunk // ROW_GROUP, body, 0)
        staged = stage_scr[s % 2].reshape(chunk, stage_scr.shape[3])
        rows_scr[s * chunk:(s + 1) * chunk, :] = staged.astype(BF16)

    @pl.when(j == 0)
    def _():
        pl.when(valid_chunks > 0)(functools.partial(issue, 0))
        for s in range(n_chunks):
            if s + 1 < n_chunks:
                pl.when(s + 1 < valid_chunks)(functools.partial(issue, s + 1))
            pl.when(s < valid_chunks)(functools.partial(drain, s))
        pl.when(t > 0)(lambda: out_copy(t - 1).wait())
        acc_scr[...] = jnp.zeros_like(acc_scr)

    def sub_tile(r0, n_rows, wg, wu, wo):
        act = _swiglu_act(rows_scr[pl.ds(r0, n_rows), :], wg, wu)
        acc_scr[pl.ds(r0, n_rows), :] += _dot(act, wo)

    def from_scratch(r0, n_rows):
        sub_tile(pl.multiple_of(r0, half), n_rows, wg_scr[...], wu_scr[...], wo_scr[...])

    def cast_weights():
        wg = wg_ref[0].astype(BF16)
        wu = wu_ref[0].astype(BF16)
        wo = wo_ref[0].astype(BF16)
        wg_scr[...] = wg
        wu_scr[...] = wu
        wo_scr[...] = wo
        return wg, wu, wo

    n_full = lax.shift_right_logical(n_valid, 1)
    has_half = (n_valid & 1) == 1

    @pl.when(n_full > 0)
    def _():
        sub_tile(0, sub, *cast_weights())

    @pl.when((n_full == 0) & has_half)
    def _():
        cast_weights()

    n_rest = jnp.maximum(n_full - 1, 0)
    n_pairs = lax.shift_right_logical(n_rest, 1)

    def pair(p, c):
        r0 = sub + p * (2 * sub)
        from_scratch(r0, sub)
        from_scratch(r0 + sub, sub)
        return c

    lax.fori_loop(0, n_pairs, pair, 0)
    pl.when((n_rest & 1) == 1)(lambda: from_scratch(sub + n_pairs * (2 * sub), sub))
    pl.when(has_half)(lambda: from_scratch(n_full * sub, half))

    @pl.when(j == pl.num_programs(1) - 1)
    def _():
        out_copy(t).start()
        pl.when(t == pl.num_programs(0) - 1)(lambda: out_copy(t).wait())


def _moe_grouped(h_f32, w_in, w_out, tile_expert, tile_nvalid, row_token, tile_rows, sub):
    N, D = h_f32.shape
    E, d_ff, _ = w_out.shape
    n_tiles = tile_expert.shape[0]
    tn = _tile(d_ff, 256)
    nj = d_ff // tn

    def ff_index(t, j, nv):
        return jnp.where(nv[t] > 0, j, nj - 1)

    assert sub % 16 == 0
    chunk = sub // 2
    return pl.pallas_call(
        functools.partial(_moe_kernel, sub=sub),
        grid_spec=pltpu.PrefetchScalarGridSpec(
            num_scalar_prefetch=3,
            grid=(n_tiles, nj),
            in_specs=[
                pl.BlockSpec(memory_space=pl.ANY),
                pl.BlockSpec((1, D, tn), lambda t, j, te, nv, rt: (te[t], 0, ff_index(t, j, nv))),
                pl.BlockSpec((1, D, tn), lambda t, j, te, nv, rt: (te[t], 0, ff_index(t, j, nv) + nj)),
                pl.BlockSpec((1, tn, D), lambda t, j, te, nv, rt: (te[t], ff_index(t, j, nv), 0)),
            ],
            out_specs=pl.BlockSpec(memory_space=pl.ANY),
            scratch_shapes=[
                pltpu.VMEM((tile_rows, D), BF16),
                pltpu.VMEM((2, chunk // ROW_GROUP, ROW_GROUP, D), F32),
                pltpu.VMEM((D, tn), BF16),
                pltpu.VMEM((D, tn), BF16),
                pltpu.VMEM((tn, D), BF16),
                pltpu.VMEM((tile_rows, D), F32),
                pltpu.SemaphoreType.DMA((2,)),
                pltpu.SemaphoreType.DMA(()),
            ],
        ),
        out_shape=jax.ShapeDtypeStruct((n_tiles * tile_rows, D), F32),
        compiler_params=_params("arbitrary", "arbitrary"),
        name="moe_grouped",
    )(tile_expert, tile_nvalid, row_token, h_f32, w_in, w_in, w_out)


def _combine_kernel(p0_ref, p1_ref, ys_hbm, x_ref, gf_ref, route_ref, out_ref, buf0, buf1, sem0, sem1):
    i = pl.program_id(0)
    n_steps = pl.num_programs(0)
    _, n_groups, _, D = buf0.shape
    tm = n_groups * ROW_GROUP

    def copies(step, group, u):
        slot = lax.rem(step, 2)
        r = step * tm + group * ROW_GROUP + u
        c0 = pltpu.make_async_copy(ys_hbm.at[pl.ds(p0_ref[r], 1), :],
                                   buf0.at[slot, group, pl.ds(u, 1), :], sem0.at[slot])
        c1 = pltpu.make_async_copy(ys_hbm.at[pl.ds(p1_ref[r], 1), :],
                                   buf1.at[slot, group, pl.ds(u, 1), :], sem1.at[slot])
        return c0, c1

    def issue(step):
        def body(group, c):
            for u in range(ROW_GROUP):
                c0, c1 = copies(step, group, u)
                c0.start()
                c1.start()
            return c
        lax.fori_loop(0, n_groups, body, 0)

    def drain(step):
        def body(group, c):
            for u in range(ROW_GROUP):
                c0, c1 = copies(step, group, u)
                c0.wait()
                c1.wait()
            return c
        lax.fori_loop(0, n_groups, body, 0)

    pl.when(i == 0)(lambda: issue(i))
    pl.when(i + 1 < n_steps)(lambda: issue(i + 1))
    drain(i)
    slot = lax.rem(i, 2)
    w0 = route_ref[:, MOE_TOPK:MOE_TOPK + 1]
    w1 = route_ref[:, MOE_TOPK + 1:MOE_TOPK + 2]
    y0 = buf0[slot].reshape(tm, D)
    y1 = buf1[slot].reshape(tm, D)
    out_ref[...] = x_ref[...] + gf_ref[0, 0] * (w0 * y0 + w1 * y1)


def _moe_combine(y_sorted, pos0, pos1, x2d, mod_l, route, seq):
    N, D = x2d.shape
    tm = _tile(seq, 256)
    rows_per_batch = seq // tm
    return pl.pallas_call(
        _combine_kernel,
        grid_spec=pltpu.PrefetchScalarGridSpec(
            num_scalar_prefetch=2,
            grid=(N // tm,),
            in_specs=[
                pl.BlockSpec(memory_space=pl.ANY),
                pl.BlockSpec((tm, D), lambda i, p0, p1: (i, 0)),
                pl.BlockSpec((1, 1, 1, D), lambda i, p0, p1: (i // rows_per_batch, 5, 0, 0)),
                pl.BlockSpec((tm, LANES), lambda i, p0, p1: (i, 0)),
            ],
            out_specs=pl.BlockSpec((tm, D), lambda i, p0, p1: (i, 0)),
            scratch_shapes=[
                pltpu.VMEM((2, tm // ROW_GROUP, ROW_GROUP, D), F32),
                pltpu.VMEM((2, tm // ROW_GROUP, ROW_GROUP, D), F32),
                pltpu.SemaphoreType.DMA((2,)),
                pltpu.SemaphoreType.DMA((2,)),
            ],
        ),
        out_shape=jax.ShapeDtypeStruct((N, D), F32),
        compiler_params=_params("arbitrary"),
        name="moe_combine",
    )(pos0, pos1, y_sorted, x2d, mod_l, route)


def _moe_tiling(n_assign, n_experts):
    per_expert = n_assign // n_experts
    sub = _tile(per_expert, 512)
    tile_rows = sub * (-(-(per_expert // 2 + sub) // sub))
    n_tiles = n_assign // tile_rows + n_experts
    return sub, tile_rows, n_tiles


def _routing_tables(route, n_experts, sub, tile_rows, n_tiles):
    N = route.shape[0]
    A = N * MOE_TOPK
    expert = route[:, :MOE_TOPK].astype(jnp.int32).reshape(A)
    onehot = (expert[:, None] == jnp.arange(n_experts, dtype=jnp.int32)[None, :]).astype(jnp.int32)
    csum = jnp.cumsum(onehot, axis=0)
    rank = jnp.sum(onehot * csum, axis=1) - 1
    counts = csum[-1]
    tiles_per = (counts + tile_rows - 1) // tile_rows
    tile_end = jnp.cumsum(tiles_per)
    tile_start = tile_end - tiles_per
    dest = tile_start[expert] * tile_rows + rank
    row_token = jnp.zeros((n_tiles * tile_rows,), jnp.int32).at[dest].set(
        jnp.arange(A, dtype=jnp.int32) // MOE_TOPK)
    used = tile_end[-1]
    tix = jnp.arange(n_tiles, dtype=jnp.int32)
    clipped = jnp.minimum(tix, used - 1)
    tile_expert = jnp.sum((clipped[:, None] >= tile_end[None, :]).astype(jnp.int32), axis=1)
    tile_expert = jnp.minimum(tile_expert, n_experts - 1).astype(jnp.int32)
    rows_left = counts[tile_expert] - (tix - tile_start[tile_expert]) * tile_rows
    half = sub // 2
    n_valid = jnp.clip((rows_left + half - 1) // half, 0, tile_rows // half)
    tile_nvalid = jnp.where(tix < used, n_valid, 0).astype(jnp.int32)
    dest2 = dest.reshape(N, MOE_TOPK)
    return tile_expert, tile_nvalid, row_token, dest2[:, 0], dest2[:, 1]


def kernel(x, c, w_ada, b_ada, attn_norm, ffn_norm, w_qkv, w_o, qk_norm_q, qk_norm_k,
           ffn_w_in, ffn_w_out, router_w, moe_w_in, moe_w_out):
    B, S, D = x.shape
    depth = w_ada.shape[0]
    H = D // HEAD_DIM
    N = B * S
    mod = _adaln(c, w_ada, b_ada).reshape(depth, B, 6, 1, D)
    x2d = x.reshape(N, D)
    for i in range(depth):
        mod_l = mod[i]
        wqkv = w_qkv[i].astype(BF16)
        wo = w_o[i].astype(BF16)
        if i % 2 == 0:
            qkv = _qkv_proj(x2d, attn_norm[i], mod_l, wqkv, S, H)
            o = _stickbreak_attention(qkv.reshape(B, S, 3 * D), H)
            x2d, h = _outproj(o.reshape(N, D), wo, x2d, ffn_norm[i], mod_l, S, BF16)
            x2d = _dense_ffn(h, ffn_w_in[i // 2].astype(BF16), ffn_w_out[i // 2].astype(BF16),
                             x2d, mod_l, S)
        else:
            jj = i // 2
            qkv = _qkv_proj(x2d, attn_norm[i], mod_l, wqkv, S, H, (qk_norm_q[jj], qk_norm_k[jj]))
            o = _moba_attention(qkv.reshape(B, S, 3 * D), H)
            x2d, h, route = _outproj(o.reshape(N, D), wo, x2d, ffn_norm[i], mod_l, S, F32,
                                     router_w[jj])
            n_experts = router_w.shape[-1]
            sub, tile_rows, n_tiles = _moe_tiling(N * MOE_TOPK, n_experts)
            te, nv, row_token, pos0, pos1 = _routing_tables(route, n_experts, sub, tile_rows, n_tiles)
            y_sorted = _moe_grouped(h, moe_w_in[jj], moe_w_out[jj], te, nv, row_token, tile_rows, sub)
            x2d = _moe_combine(y_sorted, pos0, pos1, x2d, mod_l, route, S)
    return x2d.reshape(B, S, D)
```
